```python
import math
import jax
import jax.numpy as jnp
from jax import lax
import numpy as np

D_MODEL = 2048
BATCH = 16
SEQ = 256
DEPTH = 4
DEC_BATCH = 4
DEC_SEQ = 4096
PAST_LEN = 512

GRID_W = 64
RG_WIDTH = D_MODEL // 4
RG_BLOCKS = 8
RG_BLOCK_W = RG_WIDTH // RG_BLOCKS
RG_CONV = 4
RG_C = 8.0
DA_WIDTH = D_MODEL // 2
DA_HEADS = 8
DA_V_DIM = DA_WIDTH // DA_HEADS
DA_QK_DIM = DA_V_DIM // 2
ML_WIDTH = D_MODEL // 4
ML_HEADS = 4
ML_DH = ML_WIDTH // ML_HEADS
ML_CHUNK = 64
N_EXPERTS = 32
TOP_K = 4
MOE_FF = D_MODEL
SWIGLU_LIMIT = 7.0
SWIGLU_ALPHA = 1.702
MOE_BLOCK = 128
Q_BLOCK = 128
ROPE_THETA = 10000.0
EPS = 1e-6
IN_SIZES = (RG_WIDTH, RG_WIDTH, DA_WIDTH, DA_WIDTH, DA_WIDTH, ML_WIDTH, ML_WIDTH, ML_WIDTH, ML_WIDTH, 4 * ML_HEADS)
N_IN = 2 * RG_WIDTH + 3 * DA_WIDTH + 4 * ML_WIDTH + 4 * ML_HEADS

kernel_name = 'hybrid_diffusion_rglru_diffattn_mlstm_moe_step'


def rmsnorm(x, g):
    x32 = x.astype(jnp.float32)
    y = x32 * lax.rsqrt(jnp.mean(x32 * x32, axis=-1, keepdims=True) + EPS)
    return (y * g.astype(jnp.float32)).astype(x.dtype)


def project(h, w_in):
    z = h @ w_in
    parts, off = [], 0
    for s in IN_SIZES:
        parts.append(z[..., off:off + s])
        off += s
    return parts


def depthwise_conv(x, w, b):
    y = lax.conv_general_dilated(
        x, w[:, None, :].astype(x.dtype), window_strides=(1,),
        padding=[(RG_CONV // 2, RG_CONV - 1 - RG_CONV // 2)],
        dimension_numbers=('NWC', 'WIO', 'NWC'), feature_group_count=x.shape[-1])
    return y + b.astype(x.dtype)


def linear_scan(a, u, h0):
    def combine(l, r):
        return (l[0] * r[0], r[0] * l[1] + r[1])
    A, U = lax.associative_scan(combine, (a, u), axis=1)
    h = A * h0[:, None, :] + U
    return h, h[:, -1]


def rglru_direction(xc, w_gate, b_gate, lam, h0):
    B, N, _ = xc.shape
    xb = xc.reshape(B, N, RG_BLOCKS, RG_BLOCK_W)
    g = jnp.einsum('bnkc,gkcd->gbnkd', xb, w_gate).reshape(2, B, N, RG_WIDTH) + b_gate[:, None, None, :]
    r = jax.nn.sigmoid(g[0])
    i = jax.nn.sigmoid(g[1])
    log_a = RG_C * r * jax.nn.log_sigmoid(lam)
    u = jnp.sqrt(-jnp.expm1(2.0 * log_a)) * (i * xc)
    return linear_scan(jnp.exp(log_a), u, h0)


def rglru_bidir(xc, w_gate, b_gate, lam, h0):
    hf, sf = rglru_direction(xc, w_gate[0], b_gate[0], lam[0], h0[:, 0])
    hb, sb = rglru_direction(xc[:, ::-1], w_gate[1], b_gate[1], lam[1], h0[:, 1])
    return hf + hb[:, ::-1], jnp.stack([sf, sb], axis=1)


def mlstm_direction(q, k, v, log_i, log_f, C0, n0, m0):
    B, N, H, dh = q.shape
    nc = N // ML_CHUNK
    def to_chunks(t):
        return jnp.moveaxis(t.reshape(B, nc, ML_CHUNK, *t.shape[2:]), 1, 0)
    causal = jnp.tril(jnp.ones((ML_CHUNK, ML_CHUNK), dtype=bool))[None, :, :, None]

    def step(carry, xs):
        C, n, m = carry
        qc, kc, vc, lic, lfc = xs
        b = jnp.cumsum(lfc, axis=1)
        D = b[:, :, None, :] - b[:, None, :, :] + lic[:, None, :, :]
        D = jnp.where(causal, D, -jnp.inf)
        inter = b + m[:, None, :]
        m_t = jnp.maximum(inter, jnp.max(D, axis=2))
        S = jnp.einsum('bthd,bshd->btsh', qc, kc) * jnp.exp(D - m_t[:, :, None, :])
        w_inter = jnp.exp(inter - m_t)
        num = jnp.einsum('btsh,bshd->bthd', S, vc) + w_inter[..., None] * jnp.einsum('bhde,bthe->bthd', C, qc)
        den = jnp.sum(S, axis=2) + w_inter * jnp.einsum('bhe,bthe->bth', n, qc)
        h = num / jnp.maximum(jnp.abs(den), jnp.exp(-m_t))[..., None]
        bL = b[:, -1]
        g = bL[:, None, :] - b + lic
        m_new = jnp.maximum(bL + m, jnp.max(g, axis=1))
        wgt = jnp.exp(g - m_new[:, None, :])
        decay = jnp.exp(bL + m - m_new)
        C_new = decay[..., None, None] * C + jnp.einsum('bsh,bshd,bshe->bhde', wgt, vc, kc)
        n_new = decay[..., None] * n + jnp.einsum('bsh,bshe->bhe', wgt, kc)
        return (C_new, n_new, m_new), h

    xs = tuple(to_chunks(t) for t in (q, k, v, log_i, log_f))
    state, h = lax.scan(step, (C0, n0, m0), xs)
    return jnp.moveaxis(h, 0, 1).reshape(B, N, H, dh), state


def mlstm_bidir(q, k, v, gates, C0, n0, m0):
    log_i = gates[:, :, :, 0]
    log_f = jax.nn.log_sigmoid(gates[:, :, :, 1])
    hf, (Cf, nf, mf) = mlstm_direction(q, k, v, log_i[:, :, 0], log_f[:, :, 0], C0[:, 0], n0[:, 0], m0[:, 0])
    rev = lambda t: t[:, ::-1]
    hb, (Cb, nb, mb) = mlstm_direction(rev(q), rev(k), rev(v), rev(log_i[:, :, 1]), rev(log_f[:, :, 1]),
                                       C0[:, 1], n0[:, 1], m0[:, 1])
    states = (jnp.stack([Cf, Cb], axis=1), jnp.stack([nf, nb], axis=1), jnp.stack([mf, mb], axis=1))
    return hf + rev(hb), states


def axial_rope(n):
    rows = n // GRID_W
    row = jnp.repeat(jnp.arange(rows, dtype=jnp.float32), GRID_W)
    col = jnp.tile(jnp.arange(GRID_W, dtype=jnp.float32), rows)
    n_freq = DA_QK_DIM // 4
    inv = ROPE_THETA ** (-jnp.arange(n_freq, dtype=jnp.float32) / n_freq)
    ang = jnp.stack([row[:, None] * inv, col[:, None] * inv], axis=1)
    return jnp.cos(ang), jnp.sin(ang)


def apply_rope(x, cos, sin):
    xs = x.reshape(*x.shape[:-1], 2, 2, DA_QK_DIM // 4).astype(jnp.float32)
    x1, x2 = xs[..., 0, :], xs[..., 1, :]
    c = cos[None, :, None, None]
    s = sin[None, :, None, None]
    out = jnp.stack([x1 * c - x2 * s, x1 * s + x2 * c], axis=-2)
    return out.reshape(x.shape).astype(x.dtype)


def diff_attention(q, k, v, lam):
    B, Nq = q.shape[:2]
    nb = Nq // Q_BLOCK
    qb = jnp.moveaxis(q.reshape(B, nb, Q_BLOCK, *q.shape[2:]), 1, 0)
    scale = DA_QK_DIM ** -0.5
    def block(qi):
        s = jnp.einsum('bqhcd,bkhcd->bchqk', qi, k).astype(jnp.float32) * scale
        p = jax.nn.softmax(s, axis=-1)
        w = (p[:, 0] - lam * p[:, 1]).astype(v.dtype)
        return jnp.einsum('bhqk,bkhd->bqhd', w, v)
    o = lax.map(block, qb)
    return jnp.moveaxis(o, 0, 1).reshape(B, Nq, *v.shape[2:])


def moe_ffn(x, w_r, b_r, w_gu, b_gu, w_dn, b_dn):
    shp = x.shape
    xt = x.reshape(-1, shp[-1])
    T = xt.shape[0]
    logits = (xt @ w_r + b_r).astype(jnp.float32)
    top_logit, top_e = lax.top_k(logits, TOP_K)
    top_w = jax.nn.softmax(top_logit, axis=-1)
    n_assign = T * TOP_K
    flat_e = top_e.reshape(-1)
    order = jnp.argsort(flat_e)
    sorted_e = flat_e[order]
    counts = jnp.bincount(flat_e, length=N_EXPERTS)
    padded = (counts + MOE_BLOCK - 1) // MOE_BLOCK * MOE_BLOCK
    pad_end = jnp.cumsum(padded)
    pad_start = pad_end - padded
    start = jnp.cumsum(counts) - counts
    dest = pad_start[sorted_e] + jnp.arange(n_assign) - start[sorted_e]
    n_blocks = -(-n_assign // MOE_BLOCK) + N_EXPERTS
    n_rows = n_blocks * MOE_BLOCK
    row_tok = jnp.full((n_rows,), T, dtype=jnp.int32).at[dest].set((order // TOP_K).astype(jnp.int32))
    row_w = jnp.zeros((n_rows,), jnp.float32).at[dest].set(top_w.reshape(-1)[order])
    blk_e = jnp.minimum(jnp.searchsorted(pad_end, jnp.arange(n_blocks) * MOE_BLOCK, side='right'), N_EXPERTS - 1)
    x_pad = jnp.concatenate([xt, jnp.zeros((1, xt.shape[1]), xt.dtype)], axis=0)
    xb = x_pad[row_tok].reshape(n_blocks, MOE_BLOCK, xt.shape[1])
    def expert_block(args):
        xk, e = args
        gu = xk @ w_gu[e] + b_gu[e]
        gate = jnp.minimum(gu[:, :MOE_FF], SWIGLU_LIMIT)
        up = jnp.clip(gu[:, MOE_FF:], -SWIGLU_LIMIT, SWIGLU_LIMIT)
        act = gate * jax.nn.sigmoid(SWIGLU_ALPHA * gate) * (up + 1.0)
        return act @ w_dn[e] + b_dn[e]
    yb = lax.map(expert_block, (xb, blk_e)).reshape(n_rows, -1)
    y = jax.ops.segment_sum(yb * row_w[:, None].astype(yb.dtype), row_tok, num_segments=T + 1)[:T]
    return y.reshape(shp)


def token_mixers(h, p, lam_init, cache):
    f32 = jnp.float32
    B, N, _ = h.shape
    rg_x, rg_y, da_q, da_k, da_v, ml_q, ml_k, ml_v, ml_o, ml_g = project(h, p['w_in'])
    if cache is None:
        rg_h0 = jnp.zeros((B, 2, RG_WIDTH), f32)
        C0 = jnp.zeros((B, 2, ML_HEADS, ML_DH, ML_DH), f32)
        n0 = jnp.zeros((B, 2, ML_HEADS, ML_DH), f32)
        m0 = jnp.zeros((B, 2, ML_HEADS), f32)
    else:
        ctx_k, ctx_v, rg_h0, C0, n0, m0 = cache
        rg_h0, C0, n0, m0 = rg_h0.astype(f32), C0.astype(f32), n0.astype(f32), m0.astype(f32)
    xc = depthwise_conv(rg_x, p['rg_conv_w'], p['rg_conv_b']).astype(f32)
    rg_h, rg_state = rglru_bidir(xc, p['rg_gate_w'].astype(f32), p['rg_gate_b'].astype(f32),
                                 p['rg_lambda'].astype(f32), rg_h0)
    out_rg = jax.nn.gelu(rg_y) * rg_h.astype(h.dtype)
    q = da_q.reshape(B, N, DA_HEADS, 2, DA_QK_DIM)
    k = da_k.reshape(B, N, DA_HEADS, 2, DA_QK_DIM)
    v = da_v.reshape(B, N, DA_HEADS, DA_V_DIM)
    lq1, lk1, lq2, lk2 = p['da_lambda'].astype(f32)
    lam = jnp.exp(jnp.sum(lq1 * lk1)) - jnp.exp(jnp.sum(lq2 * lk2)) + lam_init
    if cache is None:
        k_all, v_all = k, v
    else:
        cos, sin = axial_rope(N)
        q = apply_rope(q, cos, sin)
        k = apply_rope(k, cos, sin)
        n_ctx = ctx_k.shape[1]
        k_all = jnp.concatenate([ctx_k.reshape(B, n_ctx, DA_HEADS, 2, DA_QK_DIM).astype(k.dtype), k], axis=1)
        v_all = jnp.concatenate([ctx_v.astype(v.dtype), v], axis=1)
    o = diff_attention(q, k_all, v_all, lam)
    out_da = (rmsnorm(o, p['da_norm_g']) * (1.0 - lam_init)).reshape(B, N, DA_WIDTH)
    mq = ml_q.reshape(B, N, ML_HEADS, ML_DH).astype(f32)
    mk = ml_k.reshape(B, N, ML_HEADS, ML_DH).astype(f32) * (ML_DH ** -0.5)
    mv = ml_v.reshape(B, N, ML_HEADS, ML_DH).astype(f32)
    gates = ml_g.reshape(B, N, 2, 2, ML_HEADS).astype(f32) + p['ml_gate_b'].astype(f32)
    ml_h, (C, n, m) = mlstm_bidir(mq, mk, mv, gates, C0, n0, m0)
    ml_h = jax.nn.sigmoid(ml_o).reshape(B, N, ML_HEADS, ML_DH) * ml_h.astype(h.dtype)
    out_ml = rmsnorm(ml_h, p['ml_norm_g'].reshape(ML_HEADS, ML_DH)).reshape(B, N, ML_WIDTH)
    y = jnp.concatenate([out_rg, out_da, out_ml], axis=-1) @ p['w_out']
    if cache is None:
        new_ctx = (k.reshape(B, N, DA_HEADS, 2 * DA_QK_DIM), v, rg_state, C, n, m)
    else:
        new_ctx = None
    return y, new_ctx


def layer(x, cond, p, lam_init, cache):
    mod = jax.nn.silu(cond) @ p['ada_w'] + p['ada_b']
    sh1, sc1, g1, sh2, sc2, g2 = jnp.split(mod[:, None, :], 6, axis=-1)
    h = rmsnorm(x, p['norm1_g']) * (1.0 + sc1) + sh1
    y, new_ctx = token_mixers(h, p, lam_init, cache)
    x = x + g1 * y
    h = rmsnorm(x, p['norm2_g']) * (1.0 + sc2) + sh2
    x = x + g2 * moe_ffn(h, p['router_w'], p['router_b'], p['moe_w_gu'], p['moe_b_gu'],
                         p['moe_w_down'], p['moe_b_down'])
    return x, new_ctx


def setup_inputs(seed: int = 0) -> dict:
    key = jax.random.key(seed)
    ks = iter(jax.random.split(key, 40))
    f32 = jnp.float32
    def nrm(shape, s):
        return s * jax.random.normal(next(ks), shape, f32)
    L = DEPTH
    D = D_MODEL
    x_prompt = nrm((BATCH, SEQ, D), 1.0)
    x_sample = nrm((DEC_BATCH, DEC_SEQ, D), 1.0)
    c = nrm((DEC_BATCH, D), 1.0)
    cache_k = nrm((DEC_BATCH, L, PAST_LEN, DA_HEADS, 2 * DA_QK_DIM), 1.0)
    cache_v = nrm((DEC_BATCH, L, PAST_LEN, DA_HEADS, DA_V_DIM), 1.0)
    state_rglru = nrm((DEC_BATCH, L, 2, RG_WIDTH), 0.5)
    state_mlstm_C = nrm((DEC_BATCH, L, 2, ML_HEADS, ML_DH, ML_DH), 0.1)
    state_mlstm_n = nrm((DEC_BATCH, L, 2, ML_HEADS, ML_DH), 0.5)
    state_mlstm_m = nrm((DEC_BATCH, L, 2, ML_HEADS), 0.5)
    c_ctx = nrm((D,), 1.0)
    ada_w = nrm((L, D, 6 * D), 0.5 * D ** -0.5)
    ada_b = nrm((L, 6 * D), 0.02)
    norm1_g = 1.0 + nrm((L, D), 0.02)
    norm2_g = 1.0 + nrm((L, D), 0.02)
    w_in = nrm((L, D, N_IN), D ** -0.5)
    rg_conv_w = nrm((L, RG_CONV, RG_WIDTH), RG_CONV ** -0.5)
    rg_conv_b = nrm((L, RG_WIDTH), 0.01)
    rg_gate_w = nrm((L, 2, 2, RG_BLOCKS, RG_BLOCK_W, RG_BLOCK_W), RG_BLOCK_W ** -0.5)
    rg_gate_b = nrm((L, 2, 2, RG_WIDTH), 0.01)
    a_c = jax.random.uniform(next(ks), (L, 2, RG_WIDTH), f32, minval=0.9, maxval=0.999) ** (1.0 / RG_C)
    rg_lambda = jnp.log(a_c) - jnp.log1p(-a_c)
    da_lambda = nrm((L, 4, DA_QK_DIM), 0.1)
    da_norm_g = 1.0 + nrm((L, DA_V_DIM), 0.02)
    ml_gate_b = nrm((L, 2, 2, ML_HEADS), 0.1) + jnp.array([0.0, 3.0], f32)[None, None, :, None]
    ml_norm_g = 1.0 + nrm((L, ML_WIDTH), 0.02)
    w_out = nrm((L, D, D), D ** -0.5)
    router_w = nrm((L, D, N_EXPERTS), D ** -0.5)
    router_b = nrm((L, N_EXPERTS), 0.01)
    moe_w_gu = nrm((L, N_EXPERTS, D, 2 * MOE_FF), D ** -0.5)
    moe_b_gu = nrm((L, N_EXPERTS, 2 * MOE_FF), 0.01)
    moe_w_down = nrm((L, N_EXPERTS, MOE_FF, D), MOE_FF ** -0.5)
    moe_b_down = nrm((L, N_EXPERTS, D), 0.01)
    final_g = 1.0 + nrm((D,), 0.02)
    return {'x_prompt': x_prompt, 'x_sample': x_sample, 'c': c,
            'cache_k': cache_k, 'cache_v': cache_v, 'state_rglru': state_rglru,
            'state_mlstm_C': state_mlstm_C, 'state_mlstm_n': state_mlstm_n, 'state_mlstm_m': state_mlstm_m,
            'c_ctx': c_ctx, 'ada_w': ada_w, 'ada_b': ada_b, 'norm1_g': norm1_g, 'norm2_g': norm2_g,
            'w_in': w_in, 'rg_conv_w': rg_conv_w, 'rg_conv_b': rg_conv_b, 'rg_gate_w': rg_gate_w,
            'rg_gate_b': rg_gate_b, 'rg_lambda': rg_lambda, 'da_lambda': da_lambda, 'da_norm_g': da_norm_g,
            'ml_gate_b': ml_gate_b, 'ml_norm_g': ml_norm_g, 'w_out': w_out,
            'router_w': router_w, 'router_b': router_b, 'moe_w_gu': moe_w_gu, 'moe_b_gu': moe_b_gu,
            'moe_w_down': moe_w_down, 'moe_b_down': moe_b_down, 'final_g': final_g}


def reference(x_prompt, x_sample, c, cache_k, cache_v, state_rglru, state_mlstm_C, state_mlstm_n,
              state_mlstm_m, c_ctx, ada_w, ada_b, norm1_g, norm2_g, w_in, rg_conv_w, rg_conv_b,
              rg_gate_w, rg_gate_b, rg_lambda, da_lambda, da_norm_g, ml_gate_b, ml_norm_g, w_out,
              router_w, router_b, moe_w_gu, moe_b_gu, moe_w_down, moe_b_down, final_g):
    y_p = x_prompt
    y_s = x_sample
    cond_ctx = c_ctx[None, :]
    ks, vs, rgs, Cs, ns, ms = [], [], [], [], [], []
    for l in range(DEPTH):
        p = {'ada_w': ada_w[l], 'ada_b': ada_b[l], 'norm1_g': norm1_g[l], 'norm2_g': norm2_g[l],
             'w_in': w_in[l], 'rg_conv_w': rg_conv_w[l], 'rg_conv_b': rg_conv_b[l],
             'rg_gate_w': rg_gate_w[l], 'rg_gate_b': rg_gate_b[l], 'rg_lambda': rg_lambda[l],
             'da_lambda': da_lambda[l], 'da_norm_g': da_norm_g[l], 'ml_gate_b': ml_gate_b[l],
             'ml_norm_g': ml_norm_g[l], 'w_out': w_out[l], 'router_w': router_w[l], 'router_b': router_b[l],
             'moe_w_gu': moe_w_gu[l], 'moe_b_gu': moe_b_gu[l], 'moe_w_down': moe_w_down[l],
             'moe_b_down': moe_b_down[l]}
        lam_init = 0.8 - 0.6 * math.exp(-0.3 * l)
        y_p, (k_l, v_l, rg_l, C_l, n_l, m_l) = layer(y_p, cond_ctx, p, lam_init, None)
        ks.append(k_l)
        vs.append(v_l)
        rgs.append(rg_l)
        Cs.append(C_l)
        ns.append(n_l)
        ms.append(m_l)
        cache_l = (cache_k[:, l], cache_v[:, l], state_rglru[:, l], state_mlstm_C[:, l],
                   state_mlstm_n[:, l], state_mlstm_m[:, l])
        y_s, _ = layer(y_s, c, p, lam_init, cache_l)
    y_prompt = rmsnorm(y_p, final_g)
    y_sample = rmsnorm(y_s, final_g)
    new_cache_k = jnp.stack(ks, axis=1)
    new_cache_v = jnp.stack(vs, axis=1)
    new_state_rglru = jnp.stack(rgs, axis=1)
    new_state_mlstm_C = jnp.stack(Cs, axis=1)
    new_state_mlstm_n = jnp.stack(ns, axis=1)
    new_state_mlstm_m = jnp.stack(ms, axis=1)
    return (y_prompt, y_sample, new_cache_k, new_cache_v, new_state_rglru, new_state_mlstm_C, new_state_mlstm_n, new_state_mlstm_m)
```

```python
import functools
import math

import jax
import jax.numpy as jnp
from jax import lax
from jax.experimental import pallas as pl
from jax.experimental.pallas import tpu as pltpu

F32 = jnp.float32
BF16 = jnp.bfloat16
HIGHEST = lax.Precision.HIGHEST

D = 2048
EPS = 1e-6
RG_W = D // 4
RG_BLOCKS = 8
RG_C = 8.0
DA_W = D // 2
DA_H = 8
DA_DV = 128
DA_DK = 64
ML_W = D // 4
ML_H = 4
ML_DH = 128
N_EXP = 32
TOP_K = 4
FF = D
SWIGLU_LIMIT = 7.0
SWIGLU_ALPHA = 1.702
GRID_W = 64
ROPE_THETA = 10000.0
N_MAIN = 2 * RG_W + 3 * DA_W + 4 * ML_W
N_GATE = 4 * ML_H

LANE = 128
SUBLANE = 8

TM_IN = 512
TN_IN = 1024
TM_OUT = 256
CHUNK = 256
TQ = 512
TK = 512
TM_MOE = 512
TF_MOE = 512
TM_EW = 512
TM_CMB = 256
VMEM_LIMIT = 56 * 1024 * 1024


def _cparams(sem):
    return pltpu.CompilerParams(dimension_semantics=sem, vmem_limit_bytes=VMEM_LIMIT)


def _rms(x, eps=EPS):
    return x * lax.rsqrt(jnp.mean(x * x, axis=-1, keepdims=True) + eps)


def _log_sigmoid(x):
    return jnp.minimum(x, 0.0) - jnp.log1p(jnp.exp(-jnp.abs(x)))


class Geo:
    def __init__(self, n_ctx, s_ctx, n_lat, s_lat):
        self.n_ctx, self.s_ctx, self.n_lat, self.s_lat = n_ctx, s_ctx, n_lat, s_lat
        self.t_ctx = n_ctx * s_ctx
        self.t_lat = n_lat * s_lat
        self.t = self.t_ctx + self.t_lat
        self.n_seq = n_ctx + n_lat
        self.cond_rows = -(-(n_lat + 1) // SUBLANE) * SUBLANE
        self.ctx_cond_row = n_lat


def _ada_kernel(c_ref, w_ref, b_ref, o_ref):
    c = c_ref[...]
    s = (c * jax.nn.sigmoid(c)).astype(BF16)
    o_ref[...] = jnp.dot(s, w_ref[...].astype(BF16), preferred_element_type=F32) + b_ref[...]


def ada_modulation(cond, ada_w, ada_b):
    n_layer, _, n6 = ada_w.shape
    rows = cond.shape[0]
    tn = 1536
    return pl.pallas_call(
        _ada_kernel,
        grid=(n_layer, n6 // tn),
        in_specs=[pl.BlockSpec((rows, D), lambda l, j: (0, 0)),
                  pl.BlockSpec((None, D, tn), lambda l, j: (l, 0, j)),
                  pl.BlockSpec((None, 1, tn), lambda l, j: (l, 0, j))],
        out_specs=pl.BlockSpec((None, rows, tn), lambda l, j: (l, 0, j)),
        out_shape=jax.ShapeDtypeStruct((n_layer, rows, n6), F32),
        compiler_params=_cparams(("arbitrary", "arbitrary")),
        name="ada_modulation",
    )(cond, ada_w, ada_b.reshape(n_layer, 1, n6))


def _cond_row(i, geo, tm):
    n_ctx_blk = geo.t_ctx // tm
    return jnp.where(i < n_ctx_blk, geo.ctx_cond_row, (i - n_ctx_blk) // (geo.s_lat // tm))


def _inproj_kernel(x_ref, mod_ref, g_ref, w_ref, wg_ref, wgt_ref, cos_ref, sin_ref,
                   z_ref, gt_ref, gtt_ref, h_scr, *, geo):
    i = pl.program_id(0)
    j = pl.program_id(1)

    @pl.when(j == 0)
    def _():
        r = _cond_row(i, geo, TM_IN)
        shift = mod_ref[pl.ds(r, 1), 0:D]
        scale = mod_ref[pl.ds(r, 1), D:2 * D]
        h = _rms(x_ref[...]) * g_ref[...] * (1.0 + scale) + shift
        h_scr[...] = h.astype(BF16)
        gt_ref[...] = jnp.dot(h, wg_ref[...], precision=HIGHEST, preferred_element_type=F32)
        gtt_ref[...] = lax.dot_general(wgt_ref[...], h, (((1,), (1,)), ((), ())),
                                       precision=HIGHEST, preferred_element_type=F32)

    z = jnp.dot(h_scr[...], w_ref[...], preferred_element_type=F32)
    q_tile0 = (2 * RG_W) // TN_IN
    is_qk = jnp.logical_and(j >= q_tile0, j < q_tile0 + (2 * DA_W) // TN_IN)

    @pl.when(is_qk)
    def _():
        c = cos_ref[...]
        s = sin_ref[...]
        lane = lax.broadcasted_iota(jnp.int32, (TM_IN, LANE), 1)
        first = (lane % (DA_DK // 2)) < (DA_DK // 4)
        for k in range(TN_IN // LANE):
            zk = z[:, k * LANE:(k + 1) * LANE]
            partner = jnp.where(first, pltpu.roll(zk, LANE - DA_DK // 4, 1), pltpu.roll(zk, DA_DK // 4, 1))
            z_ref[:, k * LANE:(k + 1) * LANE] = zk * c + partner * s

    @pl.when(jnp.logical_not(is_qk))
    def _():
        z_ref[...] = z


def in_projection(x, mod, norm_g, w_in_bf, w_gate, w_gate_t, rope_cos, rope_sin, layer, geo):
    n_i = geo.t // TM_IN
    n_j = N_MAIN // TN_IN
    n_ctx_blk = geo.t_ctx // TM_IN
    lat_blk = geo.s_lat // TM_IN

    def rope_idx(i, j):
        return (jnp.where(i < n_ctx_blk, 0, 1 + (i - n_ctx_blk) % lat_blk), 0)

    return pl.pallas_call(
        functools.partial(_inproj_kernel, geo=geo),
        grid=(n_i, n_j),
        in_specs=[pl.BlockSpec((TM_IN, D), lambda i, j: (i, 0)),
                  pl.BlockSpec((None, geo.cond_rows, 6 * D), lambda i, j: (layer, 0, 0)),
                  pl.BlockSpec((None, 1, D), lambda i, j: (layer, 0, 0)),
                  pl.BlockSpec((None, D, TN_IN), lambda i, j: (layer, 0, j)),
                  pl.BlockSpec((None, D, N_GATE), lambda i, j: (layer, 0, 0)),
                  pl.BlockSpec((None, N_GATE, D), lambda i, j: (layer, 0, 0)),
                  pl.BlockSpec((TM_IN, LANE), rope_idx),
                  pl.BlockSpec((TM_IN, LANE), rope_idx)],
        out_specs=[pl.BlockSpec((TM_IN, TN_IN), lambda i, j: (i, j)),
                   pl.BlockSpec((TM_IN, N_GATE), lambda i, j: (i, 0)),
                   pl.BlockSpec((N_GATE, TM_IN), lambda i, j: (0, i))],
        out_shape=[jax.ShapeDtypeStruct((geo.t, N_MAIN), F32),
                   jax.ShapeDtypeStruct((geo.t, N_GATE), F32),
                   jax.ShapeDtypeStruct((N_GATE, geo.t), F32)],
        scratch_shapes=[pltpu.VMEM((TM_IN, D), BF16)],
        compiler_params=_cparams(("arbitrary", "arbitrary")),
        name="in_projection",
    )(x, mod, norm_g, w_in_bf, w_gate, w_gate_t, rope_cos, rope_sin)


def rope_tables(s_lat):
    t = jnp.arange(s_lat)
    row = (t // GRID_W).astype(F32)
    col = (t % GRID_W).astype(F32)
    d = jnp.arange(LANE) % DA_DK
    axis = d // (DA_DK // 2)
    n_freq = DA_DK // 4
    inv = ROPE_THETA ** (-(d % n_freq).astype(F32) / n_freq)
    second = (d % (DA_DK // 2)) >= n_freq
    pos = jnp.where(axis[None, :] == 0, row[:, None], col[:, None])
    ang = pos * inv[None, :]
    cos = jnp.cos(ang)
    sin = jnp.sin(ang) * jnp.where(second, 1.0, -1.0)[None, :]
    cos = jnp.concatenate([jnp.ones((TM_IN, LANE), F32), cos], axis=0)
    sin = jnp.concatenate([jnp.zeros((TM_IN, LANE), F32), sin], axis=0)
    return cos, sin


def _chunk_info(g, geo):
    ncc = geo.t_ctx // CHUNK
    cps_c = geo.s_ctx // CHUNK
    cps_l = geo.s_lat // CHUNK
    is_ctx = g < ncc
    sid = jnp.where(is_ctx, g // cps_c, geo.n_ctx + (g - ncc) // cps_l)
    pos = jnp.where(is_ctx, g % cps_c, (g - ncc) % cps_l)
    last = jnp.where(is_ctx, cps_c - 1, cps_l - 1)
    return sid, pos == 0, pos == last


def _rglru_kernel(*refs, geo, rev, n_chunk):
    if rev:
        (xp_ref, x_ref, xn_ref, cw_ref, cb_ref, wg_ref, bg_ref, lam_ref, h0_ref,
         h_ref, st_ref, pad_scr, a_scr, u_scr, carry_scr) = refs
    else:
        (xp_ref, x_ref, xn_ref, y_ref, hb_ref, cw_ref, cb_ref, wg_ref, bg_ref, lam_ref, h0_ref,
         h_ref, st_ref, pad_scr, a_scr, u_scr, carry_scr) = refs
    p = pl.program_id(0)
    g = n_chunk - 1 - p if rev else p
    _, t_first, t_last = _chunk_info(g, geo)
    starts = t_last if rev else t_first

    pad_scr[0:SUBLANE, :] = jnp.where(t_first, 0.0, xp_ref[...])
    pad_scr[SUBLANE:SUBLANE + CHUNK, :] = x_ref[...]
    pad_scr[SUBLANE + CHUNK:2 * SUBLANE + CHUNK, :] = jnp.where(t_last, 0.0, xn_ref[...])
    xc = cb_ref[...]
    for tap in range(4):
        xc = xc + cw_ref[tap:tap + 1, :] * pad_scr[pl.ds(SUBLANE - 2 + tap, CHUNK), :]

    gates = jnp.dot(xc.astype(BF16), wg_ref[...], preferred_element_type=F32) + bg_ref[...]
    r = jax.nn.sigmoid(gates[:, :RG_W])
    i = jax.nn.sigmoid(gates[:, RG_W:])
    log_a = (RG_C * _log_sigmoid(lam_ref[...])) * r
    a = jnp.exp(log_a)
    u = jnp.sqrt(-jnp.tanh(log_a) * (a * a + 1.0)) * (i * xc)

    row = lax.broadcasted_iota(jnp.int32, (CHUNK, RG_W), 0) % SUBLANE
    s = 1
    while s < SUBLANE:
        shift = CHUNK - s if rev else s
        valid = (row < SUBLANE - s) if rev else (row >= s)
        a_sh = pltpu.roll(a, shift, 0)
        u_sh = pltpu.roll(u, shift, 0)
        u = jnp.where(valid, u + a * u_sh, u)
        a = jnp.where(valid, a * a_sh, a)
        s *= 2
    a_scr[...] = a
    u_scr[...] = u

    @pl.when(starts)
    def _():
        carry_scr[...] = h0_ref[...]

    n_grp = CHUNK // SUBLANE

    def body(k, carry):
        grp = n_grp - 1 - k if rev else k
        rows = pl.ds(pl.multiple_of(grp * SUBLANE, SUBLANE), SUBLANE)
        h8 = u_scr[rows, :] + a_scr[rows, :] * carry
        u_scr[rows, :] = h8
        return h8[0:1, :] if rev else h8[SUBLANE - 1:SUBLANE, :]

    carry = lax.fori_loop(0, n_grp, body, carry_scr[...])
    carry_scr[...] = carry
    st_ref[...] = carry
    if rev:
        h_ref[...] = u_scr[...]
    else:
        h_ref[...] = (jax.nn.gelu(y_ref[...]) * (u_scr[...] + hb_ref[...])).astype(BF16)


def rglru_direction(z, hb, conv_w, conv_b, gate_w_bf, gate_b, lam, h0, layer, geo, rev):
    n_chunk = geo.t // CHUNK
    d = 1 if rev else 0
    cpb = CHUNK // SUBLANE
    n_blk8 = geo.t // SUBLANE

    def gi(p):
        return n_chunk - 1 - p if rev else p

    def sid(p):
        return _chunk_info(gi(p), geo)[0]

    x_specs = [pl.BlockSpec((SUBLANE, RG_W), lambda p: (jnp.maximum(gi(p) * cpb - 1, 0), 0)),
               pl.BlockSpec((CHUNK, RG_W), lambda p: (gi(p), 0)),
               pl.BlockSpec((SUBLANE, RG_W), lambda p: (jnp.minimum((gi(p) + 1) * cpb, n_blk8 - 1), 0))]
    w_specs = [pl.BlockSpec((None, 4, RG_W), lambda p: (layer, 0, 0)),
               pl.BlockSpec((None, 1, RG_W), lambda p: (layer, 0, 0)),
               pl.BlockSpec((None, None, RG_W, 2 * RG_W), lambda p: (layer, d, 0, 0)),
               pl.BlockSpec((None, None, 1, 2 * RG_W), lambda p: (layer, d, 0, 0)),
               pl.BlockSpec((None, None, 1, RG_W), lambda p: (layer, d, 0, 0)),
               pl.BlockSpec((None, None, 1, RG_W), lambda p: (sid(p), d, 0, 0))]
    if rev:
        in_specs = x_specs + w_specs
        args = (z, z, z, conv_w, conv_b, gate_w_bf, gate_b, lam, h0)
        h_dtype = F32
    else:
        in_specs = x_specs + [pl.BlockSpec((CHUNK, RG_W), lambda p: (gi(p), 1)),
                              pl.BlockSpec((CHUNK, RG_W), lambda p: (gi(p), 0))] + w_specs
        args = (z, z, z, z, hb, conv_w, conv_b, gate_w_bf, gate_b, lam, h0)
        h_dtype = BF16
    return pl.pallas_call(
        functools.partial(_rglru_kernel, geo=geo, rev=rev, n_chunk=n_chunk),
        grid=(n_chunk,),
        in_specs=in_specs,
        out_specs=[pl.BlockSpec((CHUNK, RG_W), lambda p: (gi(p), 0)),
                   pl.BlockSpec((None, 1, RG_W), lambda p: (sid(p), 0, 0))],
        out_shape=[jax.ShapeDtypeStruct((geo.t, RG_W), h_dtype),
                   jax.ShapeDtypeStruct((geo.n_seq, 1, RG_W), F32)],
        scratch_shapes=[pltpu.VMEM((CHUNK + 2 * SUBLANE, RG_W), F32),
                        pltpu.VMEM((CHUNK, RG_W), F32),
                        pltpu.VMEM((CHUNK, RG_W), F32),
                        pltpu.VMEM((1, RG_W), F32)],
        compiler_params=_cparams(("arbitrary",)),
        name="rglru_bwd" if rev else "rglru_fwd",
    )(*args)


def _attn_kernel(*refs, n_cache_blk, n_kv, lam_init):
    if n_cache_blk:
        q_ref, kc_ref, vc_ref, k_ref, v_ref, lam_ref, g_ref, _, o_ref, m_scr, l_scr, acc_scr = refs
    else:
        q_ref, k_ref, v_ref, lam_ref, g_ref, o_ref, m_scr, l_scr, acc_scr = refs
    ki = pl.program_id(3)

    @pl.when(ki == 0)
    def _():
        m_scr[...] = jnp.full(m_scr.shape, -jnp.inf, F32)
        l_scr[...] = jnp.zeros(l_scr.shape, F32)
        acc_scr[...] = jnp.zeros(acc_scr.shape, F32)

    def step(kr, vr):
        q = q_ref[...] * (DA_DK ** -0.5)
        lane = lax.broadcasted_iota(jnp.int32, q.shape, 1)
        kb = kr[...].astype(BF16)
        vb = vr[...].astype(BF16)
        for c in range(2):
            in_comp = (lane < DA_DK) if c == 0 else (lane >= DA_DK)
            qc = jnp.where(in_comp, q, 0.0).astype(BF16)
            s = lax.dot_general(qc, kb, (((1,), (1,)), ((), ())), preferred_element_type=F32)
            m_prev = m_scr[c]
            m_new = jnp.maximum(m_prev, jnp.max(s, axis=-1, keepdims=True))
            p = jnp.exp(s - m_new)
            alpha = jnp.exp(m_prev - m_new)
            l_scr[c] = alpha * l_scr[c] + jnp.sum(p, axis=-1, keepdims=True)
            acc_scr[c] = alpha * acc_scr[c] + jnp.dot(p.astype(BF16), vb, preferred_element_type=F32)
            m_scr[c] = m_new

    if n_cache_blk:
        pl.when(ki < n_cache_blk)(lambda: step(kc_ref, vc_ref))
        pl.when(ki >= n_cache_blk)(lambda: step(k_ref, v_ref))
    else:
        step(k_ref, v_ref)

    @pl.when(ki == n_kv - 1)
    def _():
        lq = lam_ref[...]
        lam = (jnp.exp(jnp.sum(lq[0:1] * lq[1:2], axis=-1, keepdims=True))
               - jnp.exp(jnp.sum(lq[2:3] * lq[3:4], axis=-1, keepdims=True)) + lam_init)
        o = acc_scr[0] / l_scr[0] - lam * (acc_scr[1] / l_scr[1])
        o_ref[...] = (_rms(o) * g_ref[...] * (1.0 - lam_init)).astype(BF16)


_Q_COL = (2 * RG_W) // LANE
_K_COL = _Q_COL + DA_W // LANE
_V_COL = _K_COL + DA_W // LANE


def attention_context(z, da_lambda, da_norm_g, layer, lam_init, geo):
    s = geo.s_ctx
    scratch = [pltpu.VMEM((2, s, 1), F32), pltpu.VMEM((2, s, 1), F32), pltpu.VMEM((2, s, DA_DV), F32)]
    return pl.pallas_call(
        functools.partial(_attn_kernel, n_cache_blk=0, n_kv=1, lam_init=lam_init),
        grid=(geo.n_ctx, DA_H, 1, 1),
        in_specs=[pl.BlockSpec((s, LANE), lambda b, h, qi, ki: (b, _Q_COL + h)),
                  pl.BlockSpec((s, LANE), lambda b, h, qi, ki: (b, _K_COL + h)),
                  pl.BlockSpec((s, LANE), lambda b, h, qi, ki: (b, _V_COL + h)),
                  pl.BlockSpec((None, 4, DA_DK), lambda b, h, qi, ki: (layer, 0, 0)),
                  pl.BlockSpec((None, 1, DA_DV), lambda b, h, qi, ki: (layer, 0, 0))],
        out_specs=pl.BlockSpec((s, LANE), lambda b, h, qi, ki: (b, h)),
        out_shape=jax.ShapeDtypeStruct((geo.t, DA_W), BF16),
        scratch_shapes=scratch,
        compiler_params=_cparams(("arbitrary",) * 4),
        name="attention_context",
    )(z, z, z, da_lambda, da_norm_g)


def attention_latent(z, cache_k, cache_v, out_da, da_lambda, da_norm_g, layer, lam_init, geo):
    past = cache_k.shape[2]
    n_cache_blk = past // TK
    n_kv = n_cache_blk + geo.s_lat // TK
    nq = geo.s_lat // TQ
    q0 = geo.t_ctx // TQ
    k0 = geo.t_ctx // TK

    def kv_row(b, ki):
        return k0 + b * (geo.s_lat // TK) + jnp.maximum(ki - n_cache_blk, 0)

    def cache_idx(b, h, qi, ki):
        return (b, layer, jnp.minimum(ki, n_cache_blk - 1), h)

    scratch = [pltpu.VMEM((2, TQ, 1), F32), pltpu.VMEM((2, TQ, 1), F32), pltpu.VMEM((2, TQ, DA_DV), F32)]
    return pl.pallas_call(
        functools.partial(_attn_kernel, n_cache_blk=n_cache_blk, n_kv=n_kv, lam_init=lam_init),
        grid=(geo.n_lat, DA_H, nq, n_kv),
        in_specs=[pl.BlockSpec((TQ, LANE), lambda b, h, qi, ki: (q0 + b * nq + qi, _Q_COL + h)),
                  pl.BlockSpec((None, None, TK, LANE), cache_idx),
                  pl.BlockSpec((None, None, TK, LANE), cache_idx),
                  pl.BlockSpec((TK, LANE), lambda b, h, qi, ki: (kv_row(b, ki), _K_COL + h)),
                  pl.BlockSpec((TK, LANE), lambda b, h, qi, ki: (kv_row(b, ki), _V_COL + h)),
                  pl.BlockSpec((None, 4, DA_DK), lambda b, h, qi, ki: (layer, 0, 0)),
                  pl.BlockSpec((None, 1, DA_DV), lambda b, h, qi, ki: (layer, 0, 0)),
                  pl.BlockSpec(memory_space=pl.ANY)],
        out_specs=pl.BlockSpec((TQ, LANE), lambda b, h, qi, ki: (q0 + b * nq + qi, h)),
        out_shape=jax.ShapeDtypeStruct((geo.t, DA_W), BF16),
        scratch_shapes=scratch,
        input_output_aliases={7: 0},
        compiler_params=_cparams(("arbitrary",) * 4),
        name="attention_latent",
    )(z, cache_k, cache_v, z, z, da_lambda, da_norm_g, out_da)


def _mlstm_kernel(*refs, geo, rev, n_chunk):
    if rev:
        (q_ref, k_ref, v_ref, gt_ref, gtt_ref, gb_ref, gbt_ref, c0_ref, n0_ref, m0_ref,
         h_ref, co_ref, no_ref, mo_ref, c_scr, n_scr, m_scr) = refs
    else:
        (q_ref, k_ref, v_ref, o_ref, hb_ref, gt_ref, gtt_ref, gb_ref, gbt_ref, ng_ref, c0_ref, n0_ref, m0_ref,
         h_ref, co_ref, no_ref, mo_ref, c_scr, n_scr, m_scr) = refs
    p = pl.program_id(0)
    g = n_chunk - 1 - p if rev else p
    _, t_first, t_last = _chunk_info(g, geo)
    starts = t_last if rev else t_first
    d = 1 if rev else 0

    @pl.when(starts)
    def _():
        c_scr[...] = c0_ref[...]
        n_scr[...] = n0_ref[...]
        m_scr[...] = m0_ref[...]

    gt = gt_ref[...] + gb_ref[...]
    gtt = gtt_ref[...] + gbt_ref[...]
    ti = lax.broadcasted_iota(jnp.int32, (CHUNK, CHUNK), 0)
    si = lax.broadcasted_iota(jnp.int32, (CHUNK, CHUNK), 1)
    keep = (si >= ti) if rev else (si <= ti)
    tri = jnp.where(keep, 1.0, 0.0)
    b_col_all = jnp.dot(tri, _log_sigmoid(gt), precision=HIGHEST, preferred_element_type=F32)
    tri_t = jnp.where((ti >= si) if rev else (ti <= si), 1.0, 0.0)
    b_row_all = jnp.dot(_log_sigmoid(gtt), tri_t, precision=HIGHEST, preferred_element_type=F32)
    edge = 0 if rev else CHUNK - 1

    for h in range(ML_H):
        ci = d * 2 * ML_H + h
        cf = ci + ML_H
        hs = slice(h * ML_DH, (h + 1) * ML_DH)
        qh = q_ref[:, hs]
        kh = k_ref[:, hs] * (ML_DH ** -0.5)
        vh = v_ref[:, hs]
        qb = qh.astype(BF16)
        kb = kh.astype(BF16)
        vb = vh.astype(BF16)
        li_col = gt[:, ci:ci + 1]
        li_row = gtt[ci:ci + 1, :]
        b_col = b_col_all[:, cf:cf + 1]
        b_row = b_row_all[cf:cf + 1, :]
        m_prev = m_scr[h][:, 0:1]
        c_prev = c_scr[h]
        n_prev = n_scr[h]

        dm = jnp.where(keep, b_col - b_row + li_row, -jnp.inf)
        inter = b_col + m_prev
        m_t = jnp.maximum(inter, jnp.max(dm, axis=-1, keepdims=True))
        s = lax.dot_general(qb, kb, (((1,), (1,)), ((), ())), preferred_element_type=F32) * jnp.exp(dm - m_t)
        w_inter = jnp.exp(inter - m_t)
        cq = lax.dot_general(qb, c_prev.astype(BF16), (((1,), (1,)), ((), ())), preferred_element_type=F32)
        num = jnp.dot(s.astype(BF16), vb, preferred_element_type=F32) + w_inter * cq
        den = jnp.sum(s, axis=-1, keepdims=True) + w_inter * jnp.sum(qh * n_prev, axis=-1, keepdims=True)
        hh = num / jnp.maximum(jnp.abs(den), jnp.exp(-m_t))

        b_tot = b_col[edge:edge + 1, :]
        gg = b_tot - b_col + li_col
        m_new = jnp.maximum(b_tot + m_prev, jnp.max(gg, axis=0, keepdims=True))
        wgt = jnp.exp(gg - m_new)
        decay = jnp.exp(b_tot + m_prev - m_new)
        wv_t = (wgt * vh).T.astype(BF16)
        c_new = decay * c_prev + jnp.dot(wv_t, kb, preferred_element_type=F32)
        n_new = decay * n_prev + jnp.sum(wgt * kh, axis=0, keepdims=True)
        c_scr[h] = c_new
        n_scr[h] = n_new
        m_scr[h] = jnp.broadcast_to(m_new, (1, ML_DH))
        co_ref[h] = c_new
        no_ref[h] = n_new
        mo_ref[h] = jnp.broadcast_to(m_new, (1, ML_DH))

        if rev:
            h_ref[:, hs] = hh
        else:
            gated = jax.nn.sigmoid(o_ref[:, hs]) * (hh + hb_ref[:, hs])
            h_ref[:, hs] = (_rms(gated) * ng_ref[:, hs]).astype(BF16)


_MLQ_COL = (2 * RG_W + 3 * DA_W) // ML_W


def mlstm_direction(z, hb, gates, gates_t, gate_b, gate_bt, norm_g, c0, n0, m0, layer, geo, rev):
    n_chunk = geo.t // CHUNK
    d = 1 if rev else 0

    def gi(p):
        return n_chunk - 1 - p if rev else p

    def sid(p):
        return _chunk_info(gi(p), geo)[0]

    def zspec(col):
        return pl.BlockSpec((CHUNK, ML_W), lambda p: (gi(p), col))

    qkv = [zspec(_MLQ_COL), zspec(_MLQ_COL + 1), zspec(_MLQ_COL + 2)]
    gate_specs = [pl.BlockSpec((CHUNK, N_GATE), lambda p: (gi(p), 0)),
                  pl.BlockSpec((N_GATE, CHUNK), lambda p: (0, gi(p))),
                  pl.BlockSpec((None, 1, N_GATE), lambda p: (layer, 0, 0)),
                  pl.BlockSpec((None, N_GATE, 1), lambda p: (layer, 0, 0))]
    st_specs = [pl.BlockSpec((None, None, ML_H, ML_DH, ML_DH), lambda p: (sid(p), d, 0, 0, 0)),
                pl.BlockSpec((None, None, ML_H, 1, ML_DH), lambda p: (sid(p), d, 0, 0, 0)),
                pl.BlockSpec((None, None, ML_H, 1, ML_DH), lambda p: (sid(p), d, 0, 0, 0))]
    if rev:
        in_specs = qkv + gate_specs + st_specs
        args = (z, z, z, gates, gates_t, gate_b, gate_bt, c0, n0, m0)
        h_dtype = F32
    else:
        in_specs = (qkv + [zspec(_MLQ_COL + 3), pl.BlockSpec((CHUNK, ML_W), lambda p: (gi(p), 0))] + gate_specs
                    + [pl.BlockSpec((None, 1, ML_W), lambda p: (layer, 0, 0))] + st_specs)
        args = (z, z, z, z, hb, gates, gates_t, gate_b, gate_bt, norm_g, c0, n0, m0)
        h_dtype = BF16
    return pl.pallas_call(
        functools.partial(_mlstm_kernel, geo=geo, rev=rev, n_chunk=n_chunk),
        grid=(n_chunk,),
        in_specs=in_specs,
        out_specs=[pl.BlockSpec((CHUNK, ML_W), lambda p: (gi(p), 0)),
                   pl.BlockSpec((None, ML_H, ML_DH, ML_DH), lambda p: (sid(p), 0, 0, 0)),
                   pl.BlockSpec((None, ML_H, 1, ML_DH), lambda p: (sid(p), 0, 0, 0)),
                   pl.BlockSpec((None, ML_H, 1, ML_DH), lambda p: (sid(p), 0, 0, 0))],
        out_shape=[jax.ShapeDtypeStruct((geo.t, ML_W), h_dtype),
                   jax.ShapeDtypeStruct((geo.n_seq, ML_H, ML_DH, ML_DH), F32),
                   jax.ShapeDtypeStruct((geo.n_seq, ML_H, 1, ML_DH), F32),
                   jax.ShapeDtypeStruct((geo.n_seq, ML_H, 1, ML_DH), F32)],
        scratch_shapes=[pltpu.VMEM((ML_H, ML_DH, ML_DH), F32),
                        pltpu.VMEM((ML_H, 1, ML_DH), F32),
                        pltpu.VMEM((ML_H, 1, ML_DH), F32)],
        compiler_params=_cparams(("arbitrary",)),
        name="mlstm_bwd" if rev else "mlstm_fwd",
    )(*args)


def _outproj_kernel(x_ref, rg_ref, da_ref, ml_ref, w_ref, mod_ref, g_ref, wr_ref, br_ref,
                    x1_ref, h2_ref, te_ref, tw_ref, *, geo):
    i = pl.program_id(0)
    r = _cond_row(i, geo, TM_OUT)
    gate1 = mod_ref[pl.ds(r, 1), 2 * D:3 * D]
    shift2 = mod_ref[pl.ds(r, 1), 3 * D:4 * D]
    scale2 = mod_ref[pl.ds(r, 1), 4 * D:5 * D]
    y = jnp.dot(rg_ref[...], w_ref[0:RG_W, :], preferred_element_type=F32)
    y = y + jnp.dot(da_ref[...], w_ref[RG_W:RG_W + DA_W, :], preferred_element_type=F32)
    y = y + jnp.dot(ml_ref[...], w_ref[RG_W + DA_W:D, :], preferred_element_type=F32)
    x1 = x_ref[...] + gate1 * y
    x1_ref[...] = x1
    h2 = _rms(x1) * g_ref[...] * (1.0 + scale2) + shift2
    h2_ref[...] = h2.astype(BF16)

    logits = jnp.dot(h2, wr_ref[...], precision=HIGHEST, preferred_element_type=F32) + br_ref[...]
    lane = lax.broadcasted_iota(jnp.int32, logits.shape, 1)
    lane_f = lane.astype(F32)
    sel_e = jnp.zeros(logits.shape, jnp.int32)
    sel_v = jnp.zeros(logits.shape, F32)
    top0 = None
    total = None
    for k in range(TOP_K):
        mk = jnp.max(logits, axis=-1, keepdims=True)
        idx = jnp.min(jnp.where(logits == mk, lane_f, float(LANE)), axis=-1, keepdims=True)
        if k == 0:
            top0 = mk
        ek = jnp.exp(mk - top0)
        total = ek if total is None else total + ek
        sel_e = jnp.where(lane == k, idx.astype(jnp.int32), sel_e)
        sel_v = jnp.where(lane == k, ek, sel_v)
        logits = jnp.where(lane_f == idx, -jnp.inf, logits)
    te_ref[...] = sel_e
    tw_ref[...] = sel_v / total


def out_projection(x, out_rg, out_da, out_ml, w_out_bf, mod, norm_g, router_w, router_b, layer, geo):
    n_i = geo.t // TM_OUT
    return pl.pallas_call(
        functools.partial(_outproj_kernel, geo=geo),
        grid=(n_i,),
        in_specs=[pl.BlockSpec((TM_OUT, D), lambda i: (i, 0)),
                  pl.BlockSpec((TM_OUT, RG_W), lambda i: (i, 0)),
                  pl.BlockSpec((TM_OUT, DA_W), lambda i: (i, 0)),
                  pl.BlockSpec((TM_OUT, ML_W), lambda i: (i, 0)),
                  pl.BlockSpec((None, D, D), lambda i: (layer, 0, 0)),
                  pl.BlockSpec((None, geo.cond_rows, 6 * D), lambda i: (layer, 0, 0)),
                  pl.BlockSpec((None, 1, D), lambda i: (layer, 0, 0)),
                  pl.BlockSpec((None, D, LANE), lambda i: (layer, 0, 0)),
                  pl.BlockSpec((None, 1, LANE), lambda i: (layer, 0, 0))],
        out_specs=[pl.BlockSpec((TM_OUT, D), lambda i: (i, 0)),
                   pl.BlockSpec((TM_OUT, D), lambda i: (i, 0)),
                   pl.BlockSpec((TM_OUT, LANE), lambda i: (i, 0)),
                   pl.BlockSpec((TM_OUT, LANE), lambda i: (i, 0))],
        out_shape=[jax.ShapeDtypeStruct((geo.t, D), F32),
                   jax.ShapeDtypeStruct((geo.t, D), BF16),
                   jax.ShapeDtypeStruct((geo.t, LANE), jnp.int32),
                   jax.ShapeDtypeStruct((geo.t, LANE), F32)],
        compiler_params=_cparams(("arbitrary",)),
        name="out_projection",
    )(x, out_rg, out_da, out_ml, w_out_bf, mod, norm_g, router_w, router_b)


def _moe_kernel(be_ref, nu_ref, x_ref, wg_ref, wu_ref, bg_ref, bu_ref, wd_ref, bd_ref, rw_ref, o_ref, *, n_j):
    m = pl.program_id(0)
    j = pl.program_id(1)
    used = m < nu_ref[0]

    @pl.when(used)
    def _():
        x = x_ref[...]
        gate = jnp.dot(x, wg_ref[...].astype(BF16), preferred_element_type=F32) + bg_ref[...]
        up = jnp.dot(x, wu_ref[...].astype(BF16), preferred_element_type=F32) + bu_ref[...]
        gate = jnp.minimum(gate, SWIGLU_LIMIT)
        up = jnp.clip(up, -SWIGLU_LIMIT, SWIGLU_LIMIT)
        act = gate * jax.nn.sigmoid(SWIGLU_ALPHA * gate) * (up + 1.0)
        contrib = jnp.dot(act.astype(BF16), wd_ref[...].astype(BF16), preferred_element_type=F32)

        @pl.when(j == 0)
        def _():
            o_ref[...] = contrib + bd_ref[...]

        @pl.when(j > 0)
        def _():
            o_ref[...] += contrib

        @pl.when(j == n_j - 1)
        def _():
            o_ref[...] = o_ref[...] * rw_ref[...]

    @pl.when(jnp.logical_and(jnp.logical_not(used), j == 0))
    def _():
        o_ref[...] = jnp.zeros(o_ref.shape, F32)


def moe_experts(x_sorted, row_w, blk_e, n_used, w_gu, b_gu, w_dn, b_dn, layer):
    n_rows = x_sorted.shape[0]
    n_blk = n_rows // TM_MOE
    n_j = FF // TF_MOE

    def jj(m, j, nu):
        return jnp.where(m < nu[0], j, n_j - 1)

    grid_spec = pltpu.PrefetchScalarGridSpec(
        num_scalar_prefetch=2,
        grid=(n_blk, n_j),
        in_specs=[pl.BlockSpec((TM_MOE, D), lambda m, j, be, nu: (m, 0)),
                  pl.BlockSpec((None, None, D, TF_MOE), lambda m, j, be, nu: (layer, be[m], 0, jj(m, j, nu))),
                  pl.BlockSpec((None, None, D, TF_MOE), lambda m, j, be, nu: (layer, be[m], 0, n_j + jj(m, j, nu))),
                  pl.BlockSpec((None, None, 1, TF_MOE), lambda m, j, be, nu: (layer, be[m], 0, jj(m, j, nu))),
                  pl.BlockSpec((None, None, 1, TF_MOE), lambda m, j, be, nu: (layer, be[m], 0, n_j + jj(m, j, nu))),
                  pl.BlockSpec((None, None, TF_MOE, D), lambda m, j, be, nu: (layer, be[m], jj(m, j, nu), 0)),
                  pl.BlockSpec((None, None, 1, D), lambda m, j, be, nu: (layer, be[m], 0, 0)),
                  pl.BlockSpec((TM_MOE, 1), lambda m, j, be, nu: (m, 0))],
        out_specs=pl.BlockSpec((TM_MOE, D), lambda m, j, be, nu: (m, 0)),
    )
    n_layer, n_exp = w_gu.shape[:2]
    return pl.pallas_call(
        functools.partial(_moe_kernel, n_j=n_j),
        grid_spec=grid_spec,
        out_shape=jax.ShapeDtypeStruct((n_rows, D), F32),
        compiler_params=_cparams(("arbitrary", "arbitrary")),
        name="moe_experts",
    )(blk_e, n_used, x_sorted, w_gu, w_gu, b_gu.reshape(n_layer, n_exp, 1, 2 * FF),
      b_gu.reshape(n_layer, n_exp, 1, 2 * FF), w_dn, b_dn.reshape(n_layer, n_exp, 1, D), row_w)


def moe_routing(top_e, top_w, t):
    flat_e = top_e.reshape(-1)
    n_assign = flat_e.shape[0]
    onehot = (flat_e[:, None] == jnp.arange(N_EXP, dtype=jnp.int32)[None, :]).astype(jnp.int32)
    csum = jnp.cumsum(onehot, axis=0)
    counts = csum[-1]
    pos = jnp.take_along_axis(csum, flat_e[:, None], axis=1)[:, 0] - 1
    padded = (counts + TM_MOE - 1) // TM_MOE * TM_MOE
    pad_end = jnp.cumsum(padded)
    pad_start = pad_end - padded
    dest = pad_start[flat_e] + pos
    n_blk = -(-n_assign // TM_MOE) + N_EXP
    n_rows = n_blk * TM_MOE
    row_tok = jnp.full((n_rows,), t, jnp.int32).at[dest].set(jnp.arange(n_assign, dtype=jnp.int32) // TOP_K)
    row_w = jnp.zeros((n_rows,), F32).at[dest].set(top_w.reshape(-1))
    blk_e = jnp.minimum(jnp.searchsorted(pad_end, jnp.arange(n_blk, dtype=jnp.int32) * TM_MOE, side='right'),
                        N_EXP - 1).astype(jnp.int32)
    n_used = (pad_end[-1] // TM_MOE).astype(jnp.int32).reshape(1)
    return dest, row_tok, row_w.reshape(n_rows, 1), blk_e, n_used


def _combine_kernel(x_ref, y_ref, mod_ref, o_ref, *, geo):
    r = _cond_row(pl.program_id(0), geo, TM_CMB)
    gate2 = mod_ref[pl.ds(r, 1), 5 * D:6 * D]
    y = y_ref[:, 0:D]
    for k in range(1, TOP_K):
        y = y + y_ref[:, k * D:(k + 1) * D]
    o_ref[...] = x_ref[...] + gate2 * y


def moe_combine(x1, y_gathered, mod, layer, geo):
    return pl.pallas_call(
        functools.partial(_combine_kernel, geo=geo),
        grid=(geo.t // TM_CMB,),
        in_specs=[pl.BlockSpec((TM_CMB, D), lambda i: (i, 0)),
                  pl.BlockSpec((TM_CMB, TOP_K * D), lambda i: (i, 0)),
                  pl.BlockSpec((None, geo.cond_rows, 6 * D), lambda i: (layer, 0, 0))],
        out_specs=pl.BlockSpec((TM_CMB, D), lambda i: (i, 0)),
        out_shape=jax.ShapeDtypeStruct((geo.t, D), F32),
        compiler_params=_cparams(("arbitrary",)),
        name="moe_combine",
    )(x1, y_gathered, mod)


def _final_norm_kernel(x_ref, g_ref, o_ref):
    o_ref[...] = _rms(x_ref[...]) * g_ref[...]


def final_norm(x, g):
    t = x.shape[0]
    return pl.pallas_call(
        _final_norm_kernel,
        grid=(t // TM_EW,),
        in_specs=[pl.BlockSpec((TM_EW, D), lambda i: (i, 0)), pl.BlockSpec((1, D), lambda i: (0, 0))],
        out_specs=pl.BlockSpec((TM_EW, D), lambda i: (i, 0)),
        out_shape=jax.ShapeDtypeStruct((t, D), F32),
        compiler_params=_cparams(("arbitrary",)),
        name="final_norm",
    )(x, g.reshape(1, D))


def _block_diag(w):
    eye = jnp.eye(RG_BLOCKS, dtype=w.dtype)
    dense = w[..., :, :, None, :] * eye[:, None, :, None]
    return dense.reshape(*w.shape[:-3], RG_W, RG_W)


def kernel(x_prompt, x_sample, c, cache_k, cache_v, state_rglru, state_mlstm_C, state_mlstm_n, state_mlstm_m, c_ctx, ada_w, ada_b, norm1_g, norm2_g, w_in, rg_conv_w, rg_conv_b, rg_gate_w, rg_gate_b, rg_lambda, da_lambda, da_norm_g, ml_gate_b, ml_norm_g, w_out, router_w, router_b, moe_w_gu, moe_b_gu, moe_w_down, moe_b_down, final_g):
    n_ctx, s_ctx, _ = x_prompt.shape
    n_lat, s_lat, _ = x_sample.shape
    n_layer = ada_w.shape[0]
    past = cache_k.shape[2]
    geo = Geo(n_ctx, s_ctx, n_lat, s_lat)
    assert geo.t_ctx % TM_IN == 0 and s_lat % TM_IN == 0 and s_ctx % CHUNK == 0 and s_lat % CHUNK == 0
    assert s_lat % TQ == 0 and geo.t_ctx % TQ == 0 and past % TK == 0 and s_ctx % SUBLANE == 0

    x = jnp.concatenate([x_prompt.reshape(geo.t_ctx, D), x_sample.reshape(geo.t_lat, D)], axis=0)
    cond = jnp.zeros((geo.cond_rows, D), F32).at[:n_lat].set(c).at[n_lat].set(c_ctx)
    mod = ada_modulation(cond, ada_w, ada_b)

    w_in_bf = w_in.astype(BF16)
    w_gate = w_in[:, :, N_MAIN:]
    w_gate_t = jnp.swapaxes(w_gate, 1, 2)
    w_out_bf = w_out.astype(BF16)
    rg_gate_dense = jnp.concatenate([_block_diag(rg_gate_w[:, :, 0]), _block_diag(rg_gate_w[:, :, 1])],
                                    axis=-1).astype(BF16)
    rg_gate_bias = rg_gate_b.reshape(n_layer, 2, 1, 2 * RG_W)
    rg_lam = rg_lambda.reshape(n_layer, 2, 1, RG_W)
    rg_cb = rg_conv_b.reshape(n_layer, 1, RG_W)
    ml_gb = ml_gate_b.reshape(n_layer, 1, N_GATE)
    ml_gbt = ml_gate_b.reshape(n_layer, N_GATE, 1)
    ml_ng = ml_norm_g.reshape(n_layer, 1, ML_W)
    da_ng = da_norm_g.reshape(n_layer, 1, DA_DV)
    n1g = norm1_g.reshape(n_layer, 1, D)
    n2g = norm2_g.reshape(n_layer, 1, D)
    router_w_pad = jnp.pad(router_w, ((0, 0), (0, 0), (0, LANE - N_EXP)))
    router_b_pad = jnp.pad(router_b, ((0, 0), (0, LANE - N_EXP)), constant_values=-1e30).reshape(n_layer, 1, LANE)
    rope_cos, rope_sin = rope_tables(s_lat)
    cache_k4 = cache_k.reshape(n_lat, n_layer, past, DA_W)
    cache_v4 = cache_v.reshape(n_lat, n_layer, past, DA_W)

    ks, vs, rgs, cs, ns, ms = [], [], [], [], [], []
    for l in range(n_layer):
        lam_init = 0.8 - 0.6 * math.exp(-0.3 * l)
        z, gates, gates_t = in_projection(x, mod, n1g, w_in_bf, w_gate, w_gate_t, rope_cos, rope_sin, l, geo)

        rg_h0 = jnp.concatenate([jnp.zeros((n_ctx, 2, RG_W), F32), state_rglru[:, l]], axis=0).reshape(geo.n_seq, 2, 1, RG_W)
        hb, rg_sb = rglru_direction(z, None, rg_conv_w, rg_cb, rg_gate_dense, rg_gate_bias, rg_lam, rg_h0, l, geo, True)
        out_rg, rg_sf = rglru_direction(z, hb, rg_conv_w, rg_cb, rg_gate_dense, rg_gate_bias, rg_lam, rg_h0, l, geo, False)

        out_da = attention_context(z, da_lambda, da_ng, l, lam_init, geo)
        out_da = attention_latent(z, cache_k4, cache_v4, out_da, da_lambda, da_ng, l, lam_init, geo)

        c0 = jnp.concatenate([jnp.zeros((n_ctx, 2, ML_H, ML_DH, ML_DH), F32), state_mlstm_C[:, l]], axis=0)
        n0 = jnp.concatenate([jnp.zeros((n_ctx, 2, ML_H, ML_DH), F32), state_mlstm_n[:, l]], axis=0)
        n0 = n0.reshape(geo.n_seq, 2, ML_H, 1, ML_DH)
        m0 = jnp.concatenate([jnp.zeros((n_ctx, 2, ML_H), F32), state_mlstm_m[:, l]], axis=0)
        m0 = jnp.broadcast_to(m0[..., None, None], (geo.n_seq, 2, ML_H, 1, ML_DH))
        mhb, cb, nb, mb = mlstm_direction(z, None, gates, gates_t, ml_gb, ml_gbt, ml_ng, c0, n0, m0, l, geo, True)
        out_ml, cf, nf, mf = mlstm_direction(z, mhb, gates, gates_t, ml_gb, ml_gbt, ml_ng, c0, n0, m0, l, geo, False)

        x1, h2, top_e, top_w = out_projection(x, out_rg, out_da, out_ml, w_out_bf, mod, n2g,
                                              router_w_pad, router_b_pad, l, geo)
        dest, row_tok, row_w, blk_e, n_used = moe_routing(top_e[:, :TOP_K], top_w[:, :TOP_K], geo.t)
        h2_pad = jnp.concatenate([h2, jnp.zeros((1, D), BF16)], axis=0)
        yb = moe_experts(h2_pad[row_tok], row_w, blk_e, n_used, moe_w_gu, moe_b_gu, moe_w_down, moe_b_down, l)
        x = moe_combine(x1, yb[dest].reshape(geo.t, TOP_K * D), mod, l, geo)

        ks.append(z[:geo.t_ctx, 2 * RG_W + DA_W:2 * RG_W + 2 * DA_W].reshape(n_ctx, s_ctx, DA_H, DA_DV))
        vs.append(z[:geo.t_ctx, 2 * RG_W + 2 * DA_W:2 * RG_W + 3 * DA_W].reshape(n_ctx, s_ctx, DA_H, DA_DV))
        rgs.append(jnp.stack([rg_sf[:n_ctx, 0], rg_sb[:n_ctx, 0]], axis=1))
        cs.append(jnp.stack([cf[:n_ctx], cb[:n_ctx]], axis=1))
        ns.append(jnp.stack([nf[:n_ctx, :, 0], nb[:n_ctx, :, 0]], axis=1))
        ms.append(jnp.stack([mf[:n_ctx, :, 0, 0], mb[:n_ctx, :, 0, 0]], axis=1))

    y = final_norm(x, final_g)
    y_prompt = y[:geo.t_ctx].reshape(n_ctx, s_ctx, D)
    y_sample = y[geo.t_ctx:].reshape(n_lat, s_lat, D)
    return (y_prompt, y_sample, jnp.stack(ks, axis=1), jnp.stack(vs, axis=1), jnp.stack(rgs, axis=1),
            jnp.stack(cs, axis=1), jnp.stack(ns, axis=1), jnp.stack(ms, axis=1))
```

```python
import functools
import math

import jax
import jax.numpy as jnp
from jax import lax
from jax.experimental import pallas as pl
from jax.experimental.pallas import tpu as pltpu

F32 = jnp.float32
BF16 = jnp.bfloat16
HIGHEST = lax.Precision.HIGHEST

D = 2048
EPS = 1e-6
RG_W = D // 4
RG_BLOCKS = 8
RG_C = 8.0
DA_W = D // 2
DA_H = 8
DA_DV = 128
DA_DK = 64
ML_W = D // 4
ML_H = 4
ML_DH = 128
N_EXP = 32
TOP_K = 4
FF = D
SWIGLU_LIMIT = 7.0
SWIGLU_ALPHA = 1.702
GRID_W = 64
ROPE_THETA = 10000.0
N_MAIN = 2 * RG_W + 3 * DA_W + 4 * ML_W
N_GATE = 4 * ML_H

LANE = 128
SUBLANE = 8

TM_IN = 1024
TN_IN = 1024
NORM_ROWS = 256
TM_OUT = 256
CHUNK = 256
TQ = 512
KV_STEP = 512
KV_CHUNKS = (1536, 1024, 512)
TM_MOE = 1024
TF_MOE = 256
TM_DSP = 256
TM_EW = 512
TM_CMB = 128
VMEM_LIMIT = 58 * 1024 * 1024
LOG2E = 1.4426950408889634


def _cparams(sem, **kw):
    return pltpu.CompilerParams(dimension_semantics=sem, vmem_limit_bytes=VMEM_LIMIT, **kw)


def _rms(x, eps=EPS):
    return x * lax.rsqrt(jnp.mean(x * x, axis=-1, keepdims=True) + eps)


def _log_sigmoid(x):
    return jnp.minimum(x, 0.0) - jnp.log1p(jnp.exp(-jnp.abs(x)))


class Geo:
    def __init__(self, n_ctx, s_ctx, n_lat, s_lat):
        self.n_ctx, self.s_ctx, self.n_lat, self.s_lat = n_ctx, s_ctx, n_lat, s_lat
        self.t_ctx = n_ctx * s_ctx
        self.t_lat = n_lat * s_lat
        self.t = self.t_ctx + self.t_lat
        self.n_seq = n_ctx + n_lat
        self.cond_rows = -(-(n_lat + 1) // SUBLANE) * SUBLANE
        self.ctx_cond_row = n_lat


def _ada_kernel(c_ref, w_ref, b_ref, o_ref):
    c = c_ref[...]
    s = (c * jax.nn.sigmoid(c)).astype(BF16)
    o_ref[...] = jnp.dot(s, w_ref[...].astype(BF16), preferred_element_type=F32) + b_ref[...]


def ada_modulation(cond, ada_w, ada_b):
    n_layer, _, n6 = ada_w.shape
    rows = cond.shape[0]
    tn = 1536
    return pl.pallas_call(
        _ada_kernel,
        grid=(n_layer, n6 // tn),
        in_specs=[pl.BlockSpec((rows, D), lambda l, j: (0, 0)),
                  pl.BlockSpec((None, D, tn), lambda l, j: (l, 0, j)),
                  pl.BlockSpec((None, 1, tn), lambda l, j: (l, 0, j))],
        out_specs=pl.BlockSpec((None, rows, tn), lambda l, j: (l, 0, j)),
        out_shape=jax.ShapeDtypeStruct((n_layer, rows, n6), F32),
        compiler_params=_cparams(("arbitrary", "arbitrary")),
        name="ada_modulation",
    )(cond, ada_w, ada_b.reshape(n_layer, 1, n6))


def _cond_row(i, geo, tm):
    n_ctx_blk = geo.t_ctx // tm
    return jnp.where(i < n_ctx_blk, geo.ctx_cond_row, (i - n_ctx_blk) // (geo.s_lat // tm))


def _inproj_kernel(x_ref, mod_ref, g_ref, w_ref, wg_ref, cos_ref, sin_ref,
                   z_ref, gt_ref, h_scr, *, geo):
    i = pl.program_id(0)
    j = pl.program_id(1)

    @pl.when(j == 0)
    def _():
        r = _cond_row(i, geo, TM_IN)
        gain = g_ref[...] * (1.0 + mod_ref[pl.ds(r, 1), D:2 * D])
        shift = mod_ref[pl.ds(r, 1), 0:D]
        for r0 in range(0, TM_IN, NORM_ROWS):
            rows = slice(r0, r0 + NORM_ROWS)
            h_scr[rows, :] = (_rms(x_ref[rows, :]) * gain + shift).astype(BF16)
        gt_ref[...] = jnp.dot(h_scr[...], wg_ref[...], preferred_element_type=F32)

    z = jnp.dot(h_scr[...], w_ref[...], preferred_element_type=F32)
    q_tile0 = (2 * RG_W) // TN_IN
    is_qk = jnp.logical_and(j >= q_tile0, j < q_tile0 + (2 * DA_W) // TN_IN)

    @pl.when(is_qk)
    def _():
        c = cos_ref[...]
        s = sin_ref[...]
        lane = lax.broadcasted_iota(jnp.int32, (TM_IN, LANE), 1)
        first = (lane % (DA_DK // 2)) < (DA_DK // 4)
        for k in range(TN_IN // LANE):
            zk = z[:, k * LANE:(k + 1) * LANE]
            partner = jnp.where(first, pltpu.roll(zk, LANE - DA_DK // 4, 1), pltpu.roll(zk, DA_DK // 4, 1))
            z_ref[:, k * LANE:(k + 1) * LANE] = zk * c + partner * s

    @pl.when(jnp.logical_not(is_qk))
    def _():
        z_ref[...] = z


def in_projection(x, mod, norm_g, w_in_bf, w_gate_bf, rope_cos, rope_sin, layer, geo):
    n_i = geo.t // TM_IN
    n_j = N_MAIN // TN_IN
    n_ctx_blk = geo.t_ctx // TM_IN
    lat_blk = geo.s_lat // TM_IN

    def rope_idx(i, j):
        return (jnp.where(i < n_ctx_blk, 0, 1 + (i - n_ctx_blk) % lat_blk), 0)

    return pl.pallas_call(
        functools.partial(_inproj_kernel, geo=geo),
        grid=(n_i, n_j),
        in_specs=[pl.BlockSpec((TM_IN, D), lambda i, j: (i, 0)),
                  pl.BlockSpec((None, geo.cond_rows, 6 * D), lambda i, j: (layer, 0, 0)),
                  pl.BlockSpec((None, 1, D), lambda i, j: (layer, 0, 0)),
                  pl.BlockSpec((None, D, TN_IN), lambda i, j: (layer, 0, j)),
                  pl.BlockSpec((None, D, N_GATE), lambda i, j: (layer, 0, 0)),
                  pl.BlockSpec((TM_IN, LANE), rope_idx),
                  pl.BlockSpec((TM_IN, LANE), rope_idx)],
        out_specs=[pl.BlockSpec((TM_IN, TN_IN), lambda i, j: (i, j)),
                   pl.BlockSpec((TM_IN, N_GATE), lambda i, j: (i, 0))],
        out_shape=[jax.ShapeDtypeStruct((geo.t, N_MAIN), F32),
                   jax.ShapeDtypeStruct((geo.t, N_GATE), F32)],
        scratch_shapes=[pltpu.VMEM((TM_IN, D), BF16)],
        compiler_params=_cparams(("arbitrary", "arbitrary")),
        name="in_projection",
    )(x, mod, norm_g, w_in_bf, w_gate_bf, rope_cos, rope_sin)


def rope_tables(s_lat):
    t = jnp.arange(s_lat)
    row = (t // GRID_W).astype(F32)
    col = (t % GRID_W).astype(F32)
    d = jnp.arange(LANE) % DA_DK
    axis = d // (DA_DK // 2)
    n_freq = DA_DK // 4
    inv = ROPE_THETA ** (-(d % n_freq).astype(F32) / n_freq)
    second = (d % (DA_DK // 2)) >= n_freq
    pos = jnp.where(axis[None, :] == 0, row[:, None], col[:, None])
    ang = pos * inv[None, :]
    cos = jnp.cos(ang)
    sin = jnp.sin(ang) * jnp.where(second, 1.0, -1.0)[None, :]
    cos = jnp.concatenate([jnp.ones((TM_IN, LANE), F32), cos], axis=0)
    sin = jnp.concatenate([jnp.zeros((TM_IN, LANE), F32), sin], axis=0)
    return cos, sin


def _chunk_info(g, geo):
    ncc = geo.t_ctx // CHUNK
    cps_c = geo.s_ctx // CHUNK
    cps_l = geo.s_lat // CHUNK
    is_ctx = g < ncc
    sid = jnp.where(is_ctx, g // cps_c, geo.n_ctx + (g - ncc) // cps_l)
    pos = jnp.where(is_ctx, g % cps_c, (g - ncc) % cps_l)
    last = jnp.where(is_ctx, cps_c - 1, cps_l - 1)
    return sid, pos == 0, pos == last


def _rglru_kernel(*refs, geo, rev, n_chunk):
    if rev:
        (xp_ref, x_ref, xn_ref, cw_ref, cb_ref, wg_ref, bg_ref, lam_ref, h0_ref,
         h_ref, st_ref, pad_scr, a_scr, u_scr, carry_scr) = refs
    else:
        (xp_ref, x_ref, xn_ref, y_ref, hb_ref, cw_ref, cb_ref, wg_ref, bg_ref, lam_ref, h0_ref,
         h_ref, st_ref, pad_scr, a_scr, u_scr, carry_scr) = refs
    p = pl.program_id(0)
    g = n_chunk - 1 - p if rev else p
    _, t_first, t_last = _chunk_info(g, geo)
    starts = t_last if rev else t_first

    pad_scr[0:SUBLANE, :] = jnp.where(t_first, 0.0, xp_ref[...])
    pad_scr[SUBLANE:SUBLANE + CHUNK, :] = x_ref[...]
    pad_scr[SUBLANE + CHUNK:2 * SUBLANE + CHUNK, :] = jnp.where(t_last, 0.0, xn_ref[...])
    xc = cb_ref[...]
    for tap in range(4):
        xc = xc + cw_ref[tap:tap + 1, :] * pad_scr[pl.ds(SUBLANE - 2 + tap, CHUNK), :]

    gates = jnp.dot(xc.astype(BF16), wg_ref[...], preferred_element_type=F32) + bg_ref[...]
    r = jax.nn.sigmoid(gates[:, :RG_W])
    i = jax.nn.sigmoid(gates[:, RG_W:])
    log_a = (RG_C * _log_sigmoid(lam_ref[...])) * r
    a = jnp.exp(log_a)
    u = jnp.sqrt(-jnp.tanh(log_a) * (a * a + 1.0)) * (i * xc)

    row = lax.broadcasted_iota(jnp.int32, (CHUNK, RG_W), 0) % SUBLANE
    s = 1
    while s < SUBLANE:
        shift = CHUNK - s if rev else s
        valid = (row < SUBLANE - s) if rev else (row >= s)
        a_sh = pltpu.roll(a, shift, 0)
        u_sh = pltpu.roll(u, shift, 0)
        u = jnp.where(valid, u + a * u_sh, u)
        a = jnp.where(valid, a * a_sh, a)
        s *= 2
    a_scr[...] = a
    u_scr[...] = u

    @pl.when(starts)
    def _():
        carry_scr[...] = h0_ref[...]

    n_grp = CHUNK // SUBLANE

    def body(k, carry):
        grp = n_grp - 1 - k if rev else k
        rows = pl.ds(pl.multiple_of(grp * SUBLANE, SUBLANE), SUBLANE)
        h8 = u_scr[rows, :] + a_scr[rows, :] * carry
        u_scr[rows, :] = h8
        return h8[0:1, :] if rev else h8[SUBLANE - 1:SUBLANE, :]

    carry = lax.fori_loop(0, n_grp, body, carry_scr[...])
    carry_scr[...] = carry
    st_ref[...] = carry
    if rev:
        h_ref[...] = u_scr[...]
    else:
        h_ref[...] = (jax.nn.gelu(y_ref[...]) * (u_scr[...] + hb_ref[...])).astype(BF16)


def rglru_direction(z, hb, conv_w, conv_b, gate_w_bf, gate_b, lam, h0, layer, geo, rev):
    n_chunk = geo.t // CHUNK
    d = 1 if rev else 0
    cpb = CHUNK // SUBLANE
    n_blk8 = geo.t // SUBLANE

    def gi(p):
        return n_chunk - 1 - p if rev else p

    def sid(p):
        return _chunk_info(gi(p), geo)[0]

    x_specs = [pl.BlockSpec((SUBLANE, RG_W), lambda p: (jnp.maximum(gi(p) * cpb - 1, 0), 0)),
               pl.BlockSpec((CHUNK, RG_W), lambda p: (gi(p), 0)),
               pl.BlockSpec((SUBLANE, RG_W), lambda p: (jnp.minimum((gi(p) + 1) * cpb, n_blk8 - 1), 0))]
    w_specs = [pl.BlockSpec((None, 4, RG_W), lambda p: (layer, 0, 0)),
               pl.BlockSpec((None, 1, RG_W), lambda p: (layer, 0, 0)),
               pl.BlockSpec((None, None, RG_W, 2 * RG_W), lambda p: (layer, d, 0, 0)),
               pl.BlockSpec((None, None, 1, 2 * RG_W), lambda p: (layer, d, 0, 0)),
               pl.BlockSpec((None, None, 1, RG_W), lambda p: (layer, d, 0, 0)),
               pl.BlockSpec((None, None, 1, RG_W), lambda p: (sid(p), d, 0, 0))]
    if rev:
        in_specs = x_specs + w_specs
        args = (z, z, z, conv_w, conv_b, gate_w_bf, gate_b, lam, h0)
        h_dtype = F32
    else:
        in_specs = x_specs + [pl.BlockSpec((CHUNK, RG_W), lambda p: (gi(p), 1)),
                              pl.BlockSpec((CHUNK, RG_W), lambda p: (gi(p), 0))] + w_specs
        args = (z, z, z, z, hb, conv_w, conv_b, gate_w_bf, gate_b, lam, h0)
        h_dtype = BF16
    return pl.pallas_call(
        functools.partial(_rglru_kernel, geo=geo, rev=rev, n_chunk=n_chunk),
        grid=(n_chunk,),
        in_specs=in_specs,
        out_specs=[pl.BlockSpec((CHUNK, RG_W), lambda p: (gi(p), 0)),
                   pl.BlockSpec((None, 1, RG_W), lambda p: (sid(p), 0, 0))],
        out_shape=[jax.ShapeDtypeStruct((geo.t, RG_W), h_dtype),
                   jax.ShapeDtypeStruct((geo.n_seq, 1, RG_W), F32)],
        scratch_shapes=[pltpu.VMEM((CHUNK + 2 * SUBLANE, RG_W), F32),
                        pltpu.VMEM((CHUNK, RG_W), F32),
                        pltpu.VMEM((CHUNK, RG_W), F32),
                        pltpu.VMEM((1, RG_W), F32)],
        compiler_params=_cparams(("arbitrary",)),
        name="rglru_bwd" if rev else "rglru_fwd",
    )(*args)


def _attn_kernel(*refs, past, n_new, kv_chunk, lam_init):
    if past:
        q_ref, kc_ref, vc_ref, k_ref, v_ref, lam_ref, g_ref, o_ref, kt_scr, v_scr = refs
    else:
        q_ref, k_ref, v_ref, lam_ref, g_ref, o_ref, kt_scr, v_scr = refs
    n_kv = past + n_new
    step = min(KV_STEP, n_new)

    @pl.when(pl.program_id(2) == 0)
    def _():
        for r0 in range(0, past, step):
            rows = slice(r0, r0 + step)
            kt_scr[:, rows] = kc_ref[rows, :].T.astype(BF16)
            v_scr[rows, 0:DA_DV] = vc_ref[rows, :].astype(BF16)
        for r0 in range(0, n_new, step):
            src = slice(r0, r0 + step)
            dst = slice(past + r0, past + r0 + step)
            kt_scr[:, dst] = k_ref[src, :].T.astype(BF16)
            v_scr[dst, 0:DA_DV] = v_ref[src, :].astype(BF16)
        v_scr[:, DA_DV:2 * DA_DV] = jnp.ones((n_kv, DA_DV), BF16)

    q = q_ref[...] * (DA_DK ** -0.5 * LOG2E)
    lane = lax.broadcasted_iota(jnp.int32, q.shape, 1)
    heads = []
    for c in range(2):
        in_comp = (lane < DA_DK) if c == 0 else (lane >= DA_DK)
        qc = jnp.where(in_comp, q, 0.0).astype(BF16)
        m = None
        acc = None
        for c0 in range(0, n_kv, kv_chunk):
            cols = slice(c0, c0 + kv_chunk)
            s = jnp.dot(qc, kt_scr[:, cols], preferred_element_type=F32)
            mx = jnp.max(s, axis=-1, keepdims=True)
            m_new = mx if m is None else jnp.maximum(m, mx)
            pv = jnp.dot(jnp.exp2(s - m_new).astype(BF16), v_scr[cols, :], preferred_element_type=F32)
            acc = pv if m is None else jnp.exp2(m - m_new) * acc + pv
            m = m_new
        heads.append(acc[:, 0:DA_DV] / acc[:, DA_DV:2 * DA_DV])

    lq = lam_ref[...]
    lam = (jnp.exp(jnp.sum(lq[0:1] * lq[1:2], axis=-1, keepdims=True))
           - jnp.exp(jnp.sum(lq[2:3] * lq[3:4], axis=-1, keepdims=True)) + lam_init)
    o = heads[0] - lam * heads[1]
    o_ref[...] = (_rms(o) * g_ref[...] * (1.0 - lam_init)).astype(BF16)


def _kv_chunk(n_kv):
    for cand in KV_CHUNKS:
        if n_kv % cand == 0:
            return cand
    return n_kv


def _attn_scratch(n_kv):
    return [pltpu.VMEM((2 * DA_DK, n_kv), BF16), pltpu.VMEM((n_kv, 2 * DA_DV), BF16)]


_Q_COL = (2 * RG_W) // LANE
_K_COL = _Q_COL + DA_W // LANE
_V_COL = _K_COL + DA_W // LANE


def attention_context(z, da_lambda, da_norm_g, layer, lam_init, geo):
    s = geo.s_ctx
    return pl.pallas_call(
        functools.partial(_attn_kernel, past=0, n_new=s, kv_chunk=_kv_chunk(s), lam_init=lam_init),
        grid=(geo.n_ctx, DA_H, 1),
        in_specs=[pl.BlockSpec((s, LANE), lambda b, h, qi: (b, _Q_COL + h)),
                  pl.BlockSpec((s, LANE), lambda b, h, qi: (b, _K_COL + h)),
                  pl.BlockSpec((s, LANE), lambda b, h, qi: (b, _V_COL + h)),
                  pl.BlockSpec((None, 4, DA_DK), lambda b, h, qi: (layer, 0, 0)),
                  pl.BlockSpec((None, 1, DA_DV), lambda b, h, qi: (layer, 0, 0))],
        out_specs=pl.BlockSpec((s, LANE), lambda b, h, qi: (b, h)),
        out_shape=jax.ShapeDtypeStruct((geo.t_ctx, DA_W), BF16),
        scratch_shapes=_attn_scratch(s),
        compiler_params=_cparams(("arbitrary",) * 3),
        name="attention_context",
    )(z, z, z, da_lambda, da_norm_g)


def attention_latent(z, cache_k, cache_v, da_lambda, da_norm_g, layer, lam_init, geo):
    past = cache_k.shape[2]
    s = geo.s_lat
    n_kv = past + s
    nq = s // TQ
    q0 = geo.t_ctx // TQ
    k0 = geo.t_ctx // s
    return pl.pallas_call(
        functools.partial(_attn_kernel, past=past, n_new=s, kv_chunk=_kv_chunk(n_kv), lam_init=lam_init),
        grid=(geo.n_lat, DA_H, nq),
        in_specs=[pl.BlockSpec((TQ, LANE), lambda b, h, qi: (q0 + b * nq + qi, _Q_COL + h)),
                  pl.BlockSpec((None, None, past, LANE), lambda b, h, qi: (b, layer, 0, h)),
                  pl.BlockSpec((None, None, past, LANE), lambda b, h, qi: (b, layer, 0, h)),
                  pl.BlockSpec((s, LANE), lambda b, h, qi: (k0 + b, _K_COL + h)),
                  pl.BlockSpec((s, LANE), lambda b, h, qi: (k0 + b, _V_COL + h)),
                  pl.BlockSpec((None, 4, DA_DK), lambda b, h, qi: (layer, 0, 0)),
                  pl.BlockSpec((None, 1, DA_DV), lambda b, h, qi: (layer, 0, 0))],
        out_specs=pl.BlockSpec((TQ, LANE), lambda b, h, qi: (b * nq + qi, h)),
        out_shape=jax.ShapeDtypeStruct((geo.t_lat, DA_W), BF16),
        scratch_shapes=_attn_scratch(n_kv),
        compiler_params=_cparams(("arbitrary",) * 3),
        name="attention_latent",
    )(z, cache_k, cache_v, z, z, da_lambda, da_norm_g)


def _mlstm_kernel(*refs, geo, rev, n_chunk):
    if rev:
        (q_ref, k_ref, v_ref, gt_ref, gb_ref, c0_ref, n0_ref, m0_ref,
         h_ref, co_ref, no_ref, mo_ref, c_scr, n_scr, m_scr) = refs
    else:
        (q_ref, k_ref, v_ref, o_ref, hb_ref, gt_ref, gb_ref, ng_ref, c0_ref, n0_ref, m0_ref,
         h_ref, co_ref, no_ref, mo_ref, c_scr, n_scr, m_scr) = refs
    p = pl.program_id(0)
    g = n_chunk - 1 - p if rev else p
    _, t_first, t_last = _chunk_info(g, geo)
    starts = t_last if rev else t_first
    d = 1 if rev else 0

    @pl.when(starts)
    def _():
        c_scr[...] = c0_ref[...]
        n_scr[...] = n0_ref[...]
        m_scr[...] = m0_ref[...]

    gt = gt_ref[...] + gb_ref[...]
    eye = jnp.where(lax.broadcasted_iota(jnp.int32, (N_GATE, N_GATE), 0)
                    == lax.broadcasted_iota(jnp.int32, (N_GATE, N_GATE), 1), 1.0, 0.0)
    gtt = lax.dot_general(eye, gt, (((1,), (1,)), ((), ())), precision=HIGHEST, preferred_element_type=F32)
    ti = lax.broadcasted_iota(jnp.int32, (CHUNK, CHUNK), 0)
    si = lax.broadcasted_iota(jnp.int32, (CHUNK, CHUNK), 1)
    keep = (si >= ti) if rev else (si <= ti)
    tri = jnp.where(keep, 1.0, 0.0)
    b_col_all = jnp.dot(tri, _log_sigmoid(gt), precision=HIGHEST, preferred_element_type=F32)
    tri_t = jnp.where((ti >= si) if rev else (ti <= si), 1.0, 0.0)
    b_row_all = jnp.dot(_log_sigmoid(gtt), tri_t, precision=HIGHEST, preferred_element_type=F32)
    edge = 0 if rev else CHUNK - 1

    for h in range(ML_H):
        ci = d * 2 * ML_H + h
        cf = ci + ML_H
        hs = slice(h * ML_DH, (h + 1) * ML_DH)
        qh = q_ref[:, hs]
        kh = k_ref[:, hs] * (ML_DH ** -0.5)
        vh = v_ref[:, hs]
        qb = qh.astype(BF16)
        kb = kh.astype(BF16)
        vb = vh.astype(BF16)
        li_col = gt[:, ci:ci + 1]
        li_row = gtt[ci:ci + 1, :]
        b_col = b_col_all[:, cf:cf + 1]
        b_row = b_row_all[cf:cf + 1, :]
        m_prev = m_scr[h][:, 0:1]
        c_prev = c_scr[h]
        n_prev = n_scr[h]

        dm = jnp.where(keep, b_col - b_row + li_row, -jnp.inf)
        inter = b_col + m_prev
        m_t = jnp.maximum(inter, jnp.max(dm, axis=-1, keepdims=True))
        s = lax.dot_general(qb, kb, (((1,), (1,)), ((), ())), preferred_element_type=F32) * jnp.exp(dm - m_t)
        w_inter = jnp.exp(inter - m_t)
        cq = lax.dot_general(qb, c_prev.astype(BF16), (((1,), (1,)), ((), ())), preferred_element_type=F32)
        num = jnp.dot(s.astype(BF16), vb, preferred_element_type=F32) + w_inter * cq
        den = jnp.sum(s, axis=-1, keepdims=True) + w_inter * jnp.sum(qh * n_prev, axis=-1, keepdims=True)
        hh = num / jnp.maximum(jnp.abs(den), jnp.exp(-m_t))

        b_tot = b_col[edge:edge + 1, :]
        gg = b_tot - b_col + li_col
        m_new = jnp.maximum(b_tot + m_prev, jnp.max(gg, axis=0, keepdims=True))
        wgt = jnp.exp(gg - m_new)
        decay = jnp.exp(b_tot + m_prev - m_new)
        wv_t = (wgt * vh).T.astype(BF16)
        c_new = decay * c_prev + jnp.dot(wv_t, kb, preferred_element_type=F32)
        n_new = decay * n_prev + jnp.sum(wgt * kh, axis=0, keepdims=True)
        c_scr[h] = c_new
        n_scr[h] = n_new
        m_scr[h] = jnp.broadcast_to(m_new, (1, ML_DH))
        co_ref[h] = c_new
        no_ref[h] = n_new
        mo_ref[h] = jnp.broadcast_to(m_new, (1, ML_DH))

        if rev:
            h_ref[:, hs] = hh
        else:
            gated = jax.nn.sigmoid(o_ref[:, hs]) * (hh + hb_ref[:, hs])
            h_ref[:, hs] = (_rms(gated) * ng_ref[:, hs]).astype(BF16)


_MLQ_COL = (2 * RG_W + 3 * DA_W) // ML_W


def mlstm_direction(z, hb, gates, gate_b, norm_g, c0, n0, m0, layer, geo, rev):
    n_chunk = geo.t // CHUNK
    d = 1 if rev else 0

    def gi(p):
        return n_chunk - 1 - p if rev else p

    def sid(p):
        return _chunk_info(gi(p), geo)[0]

    def zspec(col):
        return pl.BlockSpec((CHUNK, ML_W), lambda p: (gi(p), col))

    qkv = [zspec(_MLQ_COL), zspec(_MLQ_COL + 1), zspec(_MLQ_COL + 2)]
    gate_specs = [pl.BlockSpec((CHUNK, N_GATE), lambda p: (gi(p), 0)),
                  pl.BlockSpec((None, 1, N_GATE), lambda p: (layer, 0, 0))]
    st_specs = [pl.BlockSpec((None, None, ML_H, ML_DH, ML_DH), lambda p: (sid(p), d, 0, 0, 0)),
                pl.BlockSpec((None, None, ML_H, 1, ML_DH), lambda p: (sid(p), d, 0, 0, 0)),
                pl.BlockSpec((None, None, ML_H, 1, ML_DH), lambda p: (sid(p), d, 0, 0, 0))]
    if rev:
        in_specs = qkv + gate_specs + st_specs
        args = (z, z, z, gates, gate_b, c0, n0, m0)
        h_dtype = F32
    else:
        in_specs = (qkv + [zspec(_MLQ_COL + 3), pl.BlockSpec((CHUNK, ML_W), lambda p: (gi(p), 0))] + gate_specs
                    + [pl.BlockSpec((None, 1, ML_W), lambda p: (layer, 0, 0))] + st_specs)
        args = (z, z, z, z, hb, gates, gate_b, norm_g, c0, n0, m0)
        h_dtype = BF16
    return pl.pallas_call(
        functools.partial(_mlstm_kernel, geo=geo, rev=rev, n_chunk=n_chunk),
        grid=(n_chunk,),
        in_specs=in_specs,
        out_specs=[pl.BlockSpec((CHUNK, ML_W), lambda p: (gi(p), 0)),
                   pl.BlockSpec((None, ML_H, ML_DH, ML_DH), lambda p: (sid(p), 0, 0, 0)),
                   pl.BlockSpec((None, ML_H, 1, ML_DH), lambda p: (sid(p), 0, 0, 0)),
                   pl.BlockSpec((None, ML_H, 1, ML_DH), lambda p: (sid(p), 0, 0, 0))],
        out_shape=[jax.ShapeDtypeStruct((geo.t, ML_W), h_dtype),
                   jax.ShapeDtypeStruct((geo.n_seq, ML_H, ML_DH, ML_DH), F32),
                   jax.ShapeDtypeStruct((geo.n_seq, ML_H, 1, ML_DH), F32),
                   jax.ShapeDtypeStruct((geo.n_seq, ML_H, 1, ML_DH), F32)],
        scratch_shapes=[pltpu.VMEM((ML_H, ML_DH, ML_DH), F32),
                        pltpu.VMEM((ML_H, 1, ML_DH), F32),
                        pltpu.VMEM((ML_H, 1, ML_DH), F32)],
        compiler_params=_cparams(("arbitrary",)),
        name="mlstm_bwd" if rev else "mlstm_fwd",
    )(*args)


def _outproj_kernel(x_ref, rg_ref, dac_ref, dal_ref, ml_ref, w_ref, mod_ref, g_ref, wr_ref, br_ref,
                    x1_ref, h2_ref, te_ref, tw_ref, *, geo):
    i = pl.program_id(0)
    r = _cond_row(i, geo, TM_OUT)
    gate1 = mod_ref[pl.ds(r, 1), 2 * D:3 * D]
    shift2 = mod_ref[pl.ds(r, 1), 3 * D:4 * D]
    scale2 = mod_ref[pl.ds(r, 1), 4 * D:5 * D]
    da = jnp.where(i < geo.t_ctx // TM_OUT, dac_ref[...], dal_ref[...])
    y = jnp.dot(rg_ref[...], w_ref[0:RG_W, :], preferred_element_type=F32)
    y = y + jnp.dot(da, w_ref[RG_W:RG_W + DA_W, :], preferred_element_type=F32)
    y = y + jnp.dot(ml_ref[...], w_ref[RG_W + DA_W:D, :], preferred_element_type=F32)
    x1 = x_ref[...] + gate1 * y
    x1_ref[...] = x1
    h2 = _rms(x1) * g_ref[...] * (1.0 + scale2) + shift2
    lo = lax.bitcast_convert_type(h2[:, 0:D // 2].astype(BF16).astype(F32), jnp.uint32) >> 16
    hi = lax.bitcast_convert_type(h2[:, D // 2:D].astype(BF16).astype(F32), jnp.uint32)
    h2_ref[...] = hi | lo

    logits = jnp.dot(h2, wr_ref[...], precision=HIGHEST, preferred_element_type=F32) + br_ref[...]
    lane = lax.broadcasted_iota(jnp.int32, logits.shape, 1)
    lane_f = lane.astype(F32)
    sel_e = jnp.zeros(logits.shape, jnp.int32)
    sel_v = jnp.zeros(logits.shape, F32)
    top0 = None
    total = None
    for k in range(TOP_K):
        mk = jnp.max(logits, axis=-1, keepdims=True)
        idx = jnp.min(jnp.where(logits == mk, lane_f, float(LANE)), axis=-1, keepdims=True)
        if k == 0:
            top0 = mk
        ek = jnp.exp(mk - top0)
        total = ek if total is None else total + ek
        sel_e = jnp.where(lane == k, idx.astype(jnp.int32), sel_e)
        sel_v = jnp.where(lane == k, ek, sel_v)
        logits = jnp.where(lane_f == idx, -jnp.inf, logits)
    te_ref[...] = sel_e
    tw_ref[...] = sel_v / total


def out_projection(x, out_rg, da_ctx, da_lat, out_ml, w_out_bf, mod, norm_g, router_w, router_b, layer, geo):
    n_i = geo.t // TM_OUT
    n_c = geo.t_ctx // TM_OUT
    return pl.pallas_call(
        functools.partial(_outproj_kernel, geo=geo),
        grid=(n_i,),
        in_specs=[pl.BlockSpec((TM_OUT, D), lambda i: (i, 0)),
                  pl.BlockSpec((TM_OUT, RG_W), lambda i: (i, 0)),
                  pl.BlockSpec((TM_OUT, DA_W), lambda i: (jnp.minimum(i, n_c - 1), 0)),
                  pl.BlockSpec((TM_OUT, DA_W), lambda i: (jnp.maximum(i - n_c, 0), 0)),
                  pl.BlockSpec((TM_OUT, ML_W), lambda i: (i, 0)),
                  pl.BlockSpec((None, D, D), lambda i: (layer, 0, 0)),
                  pl.BlockSpec((None, geo.cond_rows, 6 * D), lambda i: (layer, 0, 0)),
                  pl.BlockSpec((None, 1, D), lambda i: (layer, 0, 0)),
                  pl.BlockSpec((None, D, LANE), lambda i: (layer, 0, 0)),
                  pl.BlockSpec((None, 1, LANE), lambda i: (layer, 0, 0))],
        out_specs=[pl.BlockSpec((TM_OUT, D), lambda i: (i, 0)),
                   pl.BlockSpec((TM_OUT, D // 2), lambda i: (i, 0)),
                   pl.BlockSpec((TM_OUT, LANE), lambda i: (i, 0)),
                   pl.BlockSpec((TM_OUT, LANE), lambda i: (i, 0))],
        out_shape=[jax.ShapeDtypeStruct((geo.t, D), F32),
                   jax.ShapeDtypeStruct((geo.t, D // 2), jnp.uint32),
                   jax.ShapeDtypeStruct((geo.t, LANE), jnp.int32),
                   jax.ShapeDtypeStruct((geo.t, LANE), F32)],
        compiler_params=_cparams(("arbitrary",)),
        name="out_projection",
    )(x, out_rg, da_ctx, da_lat, out_ml, w_out_bf, mod, norm_g, router_w, router_b)


def _dispatch_kernel(dest_ref, h_ref, xs_in, xs_out, sem):
    del xs_in

    def body(r, carry):
        for k in range(TOP_K):
            d = dest_ref[r * TOP_K + k]
            pltpu.make_async_copy(h_ref.at[pl.ds(r, 1)], xs_out.at[pl.ds(d, 1)], sem).start()
        return carry

    lax.fori_loop(0, TM_DSP, body, 0)
    for k in range(TOP_K):
        pltpu.make_async_copy(h_ref, xs_out.at[pl.ds(0, TM_DSP)], sem).wait()


def moe_dispatch(dest, h2_packed, n_rows):
    t = h2_packed.shape[0]
    return pl.pallas_call(
        _dispatch_kernel,
        grid=(t // TM_DSP,),
        in_specs=[pl.BlockSpec((TM_DSP * TOP_K,), lambda i: (i,), memory_space=pltpu.SMEM),
                  pl.BlockSpec((TM_DSP, D // 2), lambda i: (i, 0)),
                  pl.BlockSpec(memory_space=pl.ANY)],
        out_specs=pl.BlockSpec(memory_space=pl.ANY),
        out_shape=jax.ShapeDtypeStruct((n_rows, D // 2), jnp.uint32),
        scratch_shapes=[pltpu.SemaphoreType.DMA(())],
        input_output_aliases={2: 0},
        compiler_params=_cparams(("arbitrary",), disable_bounds_checks=True),
        name="moe_dispatch",
    )(dest, h2_packed, jnp.zeros((n_rows, D // 2), jnp.uint32))


def _moe_kernel(be_ref, nu_ref, x_ref, wg_ref, wu_ref, bg_ref, bu_ref, wd_ref, bd_ref, o_ref, xa_scr, xb_scr):
    m = pl.program_id(0)
    j = pl.program_id(1)
    used = m < nu_ref[0]
    half = D // 2

    @pl.when(jnp.logical_and(used, j == 0))
    def _():
        u = x_ref[...]
        xa_scr[...] = lax.bitcast_convert_type(u << 16, F32).astype(BF16)
        xb_scr[...] = lax.bitcast_convert_type((u >> 16) << 16, F32).astype(BF16)

    @pl.when(used)
    def _():
        xa = xa_scr[...]
        xb = xb_scr[...]
        wg = wg_ref[...].astype(BF16)
        wu = wu_ref[...].astype(BF16)
        gate = (jnp.dot(xa, wg[0:half], preferred_element_type=F32)
                + jnp.dot(xb, wg[half:D], preferred_element_type=F32) + bg_ref[...])
        up = (jnp.dot(xa, wu[0:half], preferred_element_type=F32)
              + jnp.dot(xb, wu[half:D], preferred_element_type=F32) + bu_ref[...])
        gate = jnp.minimum(gate, SWIGLU_LIMIT)
        up = jnp.clip(up, -SWIGLU_LIMIT, SWIGLU_LIMIT)
        act = (gate * jax.nn.sigmoid(SWIGLU_ALPHA * gate) * (up + 1.0)).astype(BF16)
        wd = wd_ref[...].astype(BF16)
        for n0 in range(0, D, half):
            cols = slice(n0, n0 + half)
            contrib = jnp.dot(act, wd[:, cols], preferred_element_type=F32)

            @pl.when(j == 0)
            def _():
                o_ref[:, cols] = contrib + bd_ref[:, cols]

            @pl.when(j > 0)
            def _():
                o_ref[:, cols] += contrib

    @pl.when(jnp.logical_and(jnp.logical_not(used), j == 0))
    def _():
        o_ref[...] = jnp.zeros(o_ref.shape, F32)


def moe_experts(x_sorted, blk_e, n_used, w_gu, b_gu, w_dn, b_dn, layer):
    n_rows = x_sorted.shape[0]
    n_blk = n_rows // TM_MOE
    n_j = FF // TF_MOE

    def jj(m, j, nu):
        return jnp.where(m < nu[0], j, n_j - 1)

    grid_spec = pltpu.PrefetchScalarGridSpec(
        num_scalar_prefetch=2,
        grid=(n_blk, n_j),
        in_specs=[pl.BlockSpec((TM_MOE, D // 2), lambda m, j, be, nu: (m, 0)),
                  pl.BlockSpec((None, None, D, TF_MOE), lambda m, j, be, nu: (layer, be[m], 0, jj(m, j, nu))),
                  pl.BlockSpec((None, None, D, TF_MOE), lambda m, j, be, nu: (layer, be[m], 0, n_j + jj(m, j, nu))),
                  pl.BlockSpec((None, None, 1, TF_MOE), lambda m, j, be, nu: (layer, be[m], 0, jj(m, j, nu))),
                  pl.BlockSpec((None, None, 1, TF_MOE), lambda m, j, be, nu: (layer, be[m], 0, n_j + jj(m, j, nu))),
                  pl.BlockSpec((None, None, TF_MOE, D), lambda m, j, be, nu: (layer, be[m], jj(m, j, nu), 0)),
                  pl.BlockSpec((None, None, 1, D), lambda m, j, be, nu: (layer, be[m], 0, 0))],
        out_specs=pl.BlockSpec((TM_MOE, D), lambda m, j, be, nu: (m, 0)),
        scratch_shapes=[pltpu.VMEM((TM_MOE, D // 2), BF16), pltpu.VMEM((TM_MOE, D // 2), BF16)],
    )
    n_layer, n_exp = w_gu.shape[:2]
    return pl.pallas_call(
        _moe_kernel,
        grid_spec=grid_spec,
        out_shape=jax.ShapeDtypeStruct((n_rows, D), F32),
        compiler_params=_cparams(("arbitrary", "arbitrary")),
        name="moe_experts",
    )(blk_e, n_used, x_sorted, w_gu, w_gu, b_gu.reshape(n_layer, n_exp, 1, 2 * FF),
      b_gu.reshape(n_layer, n_exp, 1, 2 * FF), w_dn, b_dn.reshape(n_layer, n_exp, 1, D))


def moe_routing(top_e):
    flat_e = top_e.reshape(-1)
    n_assign = flat_e.shape[0]
    onehot = (flat_e[:, None] == jnp.arange(N_EXP, dtype=jnp.int32)[None, :]).astype(jnp.int32)
    csum = jnp.cumsum(onehot, axis=0)
    counts = csum[-1]
    pos = jnp.take_along_axis(csum, flat_e[:, None], axis=1)[:, 0] - 1
    padded = (counts + TM_MOE - 1) // TM_MOE * TM_MOE
    pad_end = jnp.cumsum(padded)
    pad_start = pad_end - padded
    dest = pad_start[flat_e] + pos
    n_blk = moe_num_blocks(n_assign)
    blk_e = jnp.minimum(jnp.searchsorted(pad_end, jnp.arange(n_blk, dtype=jnp.int32) * TM_MOE, side='right'),
                        N_EXP - 1).astype(jnp.int32)
    n_used = (pad_end[-1] // TM_MOE).astype(jnp.int32).reshape(1)
    return dest.astype(jnp.int32), blk_e, n_used


def moe_num_blocks(n_assign):
    return -(-n_assign // TM_MOE) + N_EXP


def _combine_kernel(dfirst_ref, dnext_ref, x_ref, tw_ref, mod_ref, yb_ref, o_ref, buf, sem, *, geo, n_blk):
    i = pl.program_id(0)

    def gather(d_ref, slot):
        def body(r, carry):
            for k in range(TOP_K):
                d = d_ref[r * TOP_K + k]
                pltpu.make_async_copy(yb_ref.at[pl.ds(d, 1)], buf.at[slot, k, pl.ds(r, 1)], sem.at[slot]).start()
            return carry

        lax.fori_loop(0, TM_CMB, body, 0)

    @pl.when(i == 0)
    def _():
        gather(dfirst_ref, 0)

    @pl.when(i + 1 < n_blk)
    def _():
        gather(dnext_ref, (i + 1) % 2)

    slot = i % 2
    for k in range(TOP_K):
        pltpu.make_async_copy(yb_ref.at[pl.ds(0, TM_CMB)], buf.at[slot, k], sem.at[slot]).wait()

    r = _cond_row(i, geo, TM_CMB)
    gate2 = mod_ref[pl.ds(r, 1), 5 * D:6 * D]
    tw = tw_ref[...]
    y = tw[:, 0:1] * buf[slot, 0]
    for k in range(1, TOP_K):
        y = y + tw[:, k:k + 1] * buf[slot, k]
    o_ref[...] = x_ref[...] + gate2 * y


def moe_combine(dest, x1, top_w, yb, mod, layer, geo):
    n_blk = geo.t // TM_CMB
    dblk = TM_CMB * TOP_K
    return pl.pallas_call(
        functools.partial(_combine_kernel, geo=geo, n_blk=n_blk),
        grid=(n_blk,),
        in_specs=[pl.BlockSpec((dblk,), lambda i: (0,), memory_space=pltpu.SMEM),
                  pl.BlockSpec((dblk,), lambda i: (jnp.minimum(i + 1, n_blk - 1),), memory_space=pltpu.SMEM),
                  pl.BlockSpec((TM_CMB, D), lambda i: (i, 0)),
                  pl.BlockSpec((TM_CMB, LANE), lambda i: (i, 0)),
                  pl.BlockSpec((None, geo.cond_rows, 6 * D), lambda i: (layer, 0, 0)),
                  pl.BlockSpec(memory_space=pl.ANY)],
        out_specs=pl.BlockSpec((TM_CMB, D), lambda i: (i, 0)),
        out_shape=jax.ShapeDtypeStruct((geo.t, D), F32),
        scratch_shapes=[pltpu.VMEM((2, TOP_K, TM_CMB, D), F32), pltpu.SemaphoreType.DMA((2,))],
        compiler_params=_cparams(("arbitrary",), disable_bounds_checks=True),
        name="moe_combine",
    )(dest, dest, x1, top_w, mod, yb)


def _final_norm_kernel(x_ref, g_ref, o_ref):
    o_ref[...] = _rms(x_ref[...]) * g_ref[...]


def final_norm(x, g):
    t = x.shape[0]
    return pl.pallas_call(
        _final_norm_kernel,
        grid=(t // TM_EW,),
        in_specs=[pl.BlockSpec((TM_EW, D), lambda i: (i, 0)), pl.BlockSpec((1, D), lambda i: (0, 0))],
        out_specs=pl.BlockSpec((TM_EW, D), lambda i: (i, 0)),
        out_shape=jax.ShapeDtypeStruct((t, D), F32),
        compiler_params=_cparams(("arbitrary",)),
        name="final_norm",
    )(x, g.reshape(1, D))


def _block_diag(w):
    eye = jnp.eye(RG_BLOCKS, dtype=w.dtype)
    dense = w[..., :, :, None, :] * eye[:, None, :, None]
    return dense.reshape(*w.shape[:-3], RG_W, RG_W)


def kernel(x_prompt, x_sample, c, cache_k, cache_v, state_rglru, state_mlstm_C, state_mlstm_n, state_mlstm_m, c_ctx, ada_w, ada_b, norm1_g, norm2_g, w_in, rg_conv_w, rg_conv_b, rg_gate_w, rg_gate_b, rg_lambda, da_lambda, da_norm_g, ml_gate_b, ml_norm_g, w_out, router_w, router_b, moe_w_gu, moe_b_gu, moe_w_down, moe_b_down, final_g):
    n_ctx, s_ctx, _ = x_prompt.shape
    n_lat, s_lat, _ = x_sample.shape
    n_layer = ada_w.shape[0]
    past = cache_k.shape[2]
    geo = Geo(n_ctx, s_ctx, n_lat, s_lat)
    assert geo.t_ctx % TM_IN == 0 and s_lat % TM_IN == 0 and s_ctx % CHUNK == 0 and s_lat % CHUNK == 0
    assert s_lat % TQ == 0 and geo.t_ctx % s_lat == 0 and past % KV_STEP == 0 and s_ctx % SUBLANE == 0

    x = jnp.concatenate([x_prompt.reshape(geo.t_ctx, D), x_sample.reshape(geo.t_lat, D)], axis=0)
    cond = jnp.zeros((geo.cond_rows, D), F32).at[:n_lat].set(c).at[n_lat].set(c_ctx)
    mod = ada_modulation(cond, ada_w, ada_b)

    w_in_bf = w_in.astype(BF16)
    w_gate_bf = w_in_bf[:, :, N_MAIN:]
    w_out_bf = w_out.astype(BF16)
    rg_gate_dense = jnp.concatenate([_block_diag(rg_gate_w[:, :, 0]), _block_diag(rg_gate_w[:, :, 1])],
                                    axis=-1).astype(BF16)
    rg_gate_bias = rg_gate_b.reshape(n_layer, 2, 1, 2 * RG_W)
    rg_lam = rg_lambda.reshape(n_layer, 2, 1, RG_W)
    rg_cb = rg_conv_b.reshape(n_layer, 1, RG_W)
    ml_gb = ml_gate_b.reshape(n_layer, 1, N_GATE)
    ml_ng = ml_norm_g.reshape(n_layer, 1, ML_W)
    da_ng = da_norm_g.reshape(n_layer, 1, DA_DV)
    n1g = norm1_g.reshape(n_layer, 1, D)
    n2g = norm2_g.reshape(n_layer, 1, D)
    router_w_pad = jnp.pad(router_w, ((0, 0), (0, 0), (0, LANE - N_EXP)))
    router_b_pad = jnp.pad(router_b, ((0, 0), (0, LANE - N_EXP)), constant_values=-1e30).reshape(n_layer, 1, LANE)
    rope_cos, rope_sin = rope_tables(s_lat)
    cache_k4 = cache_k.reshape(n_lat, n_layer, past, DA_W)
    cache_v4 = cache_v.reshape(n_lat, n_layer, past, DA_W)

    ks, vs, rgs, cs, ns, ms = [], [], [], [], [], []
    for l in range(n_layer):
        lam_init = 0.8 - 0.6 * math.exp(-0.3 * l)
        z, gates = in_projection(x, mod, n1g, w_in_bf, w_gate_bf, rope_cos, rope_sin, l, geo)

        rg_h0 = jnp.concatenate([jnp.zeros((n_ctx, 2, RG_W), F32), state_rglru[:, l]], axis=0).reshape(geo.n_seq, 2, 1, RG_W)
        hb, rg_sb = rglru_direction(z, None, rg_conv_w, rg_cb, rg_gate_dense, rg_gate_bias, rg_lam, rg_h0, l, geo, True)
        out_rg, rg_sf = rglru_direction(z, hb, rg_conv_w, rg_cb, rg_gate_dense, rg_gate_bias, rg_lam, rg_h0, l, geo, False)

        da_ctx = attention_context(z, da_lambda, da_ng, l, lam_init, geo)
        da_lat = attention_latent(z, cache_k4, cache_v4, da_lambda, da_ng, l, lam_init, geo)

        c0 = jnp.concatenate([jnp.zeros((n_ctx, 2, ML_H, ML_DH, ML_DH), F32), state_mlstm_C[:, l]], axis=0)
        n0 = jnp.concatenate([jnp.zeros((n_ctx, 2, ML_H, ML_DH), F32), state_mlstm_n[:, l]], axis=0)
        n0 = n0.reshape(geo.n_seq, 2, ML_H, 1, ML_DH)
        m0 = jnp.concatenate([jnp.zeros((n_ctx, 2, ML_H), F32), state_mlstm_m[:, l]], axis=0)
        m0 = jnp.broadcast_to(m0[..., None, None], (geo.n_seq, 2, ML_H, 1, ML_DH))
        mhb, cb, nb, mb = mlstm_direction(z, None, gates, ml_gb, ml_ng, c0, n0, m0, l, geo, True)
        out_ml, cf, nf, mf = mlstm_direction(z, mhb, gates, ml_gb, ml_ng, c0, n0, m0, l, geo, False)

        x1, h2_packed, top_e, top_w = out_projection(x, out_rg, da_ctx, da_lat, out_ml, w_out_bf, mod, n2g,
                                                     router_w_pad, router_b_pad, l, geo)
        dest, blk_e, n_used = moe_routing(top_e[:, :TOP_K])
        x_sorted = moe_dispatch(dest, h2_packed, moe_num_blocks(geo.t * TOP_K) * TM_MOE)
        yb = moe_experts(x_sorted, blk_e, n_used, moe_w_gu, moe_b_gu, moe_w_down, moe_b_down, l)
        x = moe_combine(dest, x1, top_w, yb, mod, l, geo)

        ks.append(z[:geo.t_ctx, 2 * RG_W + DA_W:2 * RG_W + 2 * DA_W].reshape(n_ctx, s_ctx, DA_H, DA_DV))
        vs.append(z[:geo.t_ctx, 2 * RG_W + 2 * DA_W:2 * RG_W + 3 * DA_W].reshape(n_ctx, s_ctx, DA_H, DA_DV))
        rgs.append(jnp.stack([rg_sf[:n_ctx, 0], rg_sb[:n_ctx, 0]], axis=1))
        cs.append(jnp.stack([cf[:n_ctx], cb[:n_ctx]], axis=1))
        ns.append(jnp.stack([nf[:n_ctx, :, 0], nb[:n_ctx, :, 0]], axis=1))
        ms.append(jnp.stack([mf[:n_ctx, :, 0, 0], mb[:n_ctx, :, 0, 0]], axis=1))

    y = final_norm(x, final_g)
    y_prompt = y[:geo.t_ctx].reshape(n_ctx, s_ctx, D)
    y_sample = y[geo.t_ctx:].reshape(n_lat, s_lat, D)
    return (y_prompt, y_sample, jnp.stack(ks, axis=1), jnp.stack(vs, axis=1), jnp.stack(rgs, axis=1),
            jnp.stack(cs, axis=1), jnp.stack(ns, axis=1), jnp.stack(ms, axis=1))
```

```python
import functools
import math

import jax
import jax.numpy as jnp
from jax import lax
from jax.experimental import pallas as pl
from jax.experimental.pallas import tpu as pltpu

F32 = jnp.float32
BF16 = jnp.bfloat16
HIGHEST = lax.Precision.HIGHEST

D = 2048
EPS = 1e-6
RG_W = D // 4
RG_BLOCKS = 8
RG_C = 8.0
DA_W = D // 2
DA_H = 8
DA_DV = 128
DA_DK = 64
ML_W = D // 4
ML_H = 4
ML_DH = 128
N_EXP = 32
TOP_K = 4
FF = D
SWIGLU_LIMIT = 7.0
SWIGLU_ALPHA = 1.702
GRID_W = 64
ROPE_THETA = 10000.0
N_MAIN = 2 * RG_W + 3 * DA_W + 4 * ML_W
N_GATE = 4 * ML_H

LANE = 128
SUBLANE = 8

TM_IN = 1024
TN_IN = 1024
NORM_ROWS = 256
TM_OUT = 512
CHUNK = 256
TQ = 512
KV_STEP = 512
KV_CHUNKS = (1536, 1024, 512)
TM_MOE = 1024
TF_MOE = 512
TN_MOE = 512
TM_DSP = 256
TM_EW = 512
TM_CMB = 128
VMEM_LIMIT = 58 * 1024 * 1024
LOG2E = 1.4426950408889634


def _cparams(sem, **kw):
    return pltpu.CompilerParams(dimension_semantics=sem, vmem_limit_bytes=VMEM_LIMIT, **kw)


def _rms(x, eps=EPS):
    return x * lax.rsqrt(jnp.mean(x * x, axis=-1, keepdims=True) + eps)


def _log_sigmoid(x):
    return jnp.minimum(x, 0.0) - jnp.log1p(jnp.exp(-jnp.abs(x)))


class Geo:
    def __init__(self, n_ctx, s_ctx, n_lat, s_lat):
        self.n_ctx, self.s_ctx, self.n_lat, self.s_lat = n_ctx, s_ctx, n_lat, s_lat
        self.t_ctx = n_ctx * s_ctx
        self.t_lat = n_lat * s_lat
        self.t = self.t_ctx + self.t_lat
        self.n_seq = n_ctx + n_lat
        self.cond_rows = -(-(n_lat + 1) // SUBLANE) * SUBLANE
        self.ctx_cond_row = n_lat


def _ada_kernel(c_ref, w_ref, b_ref, o_ref):
    c = c_ref[...]
    s = (c * jax.nn.sigmoid(c)).astype(BF16)
    o_ref[...] = jnp.dot(s, w_ref[...].astype(BF16), preferred_element_type=F32) + b_ref[...]


def ada_modulation(cond, ada_w, ada_b):
    n_layer, _, n6 = ada_w.shape
    rows = cond.shape[0]
    tn = 1536
    return pl.pallas_call(
        _ada_kernel,
        grid=(n_layer, n6 // tn),
        in_specs=[pl.BlockSpec((rows, D), lambda l, j: (0, 0)),
                  pl.BlockSpec((None, D, tn), lambda l, j: (l, 0, j)),
                  pl.BlockSpec((None, 1, tn), lambda l, j: (l, 0, j))],
        out_specs=pl.BlockSpec((None, rows, tn), lambda l, j: (l, 0, j)),
        out_shape=jax.ShapeDtypeStruct((n_layer, rows, n6), F32),
        compiler_params=_cparams(("arbitrary", "arbitrary")),
        name="ada_modulation",
    )(cond, ada_w, ada_b.reshape(n_layer, 1, n6))


def _cond_row(i, geo, tm):
    n_ctx_blk = geo.t_ctx // tm
    return jnp.where(i < n_ctx_blk, geo.ctx_cond_row, (i - n_ctx_blk) // (geo.s_lat // tm))


def _inproj_kernel(x_ref, mod_ref, g_ref, w_ref, wg_ref, cos_ref, sin_ref,
                   z_ref, gt_ref, h_scr, *, geo):
    i = pl.program_id(0)
    j = pl.program_id(1)

    @pl.when(j == 0)
    def _():
        r = _cond_row(i, geo, TM_IN)
        gain = g_ref[...] * (1.0 + mod_ref[pl.ds(r, 1), D:2 * D])
        shift = mod_ref[pl.ds(r, 1), 0:D]
        for r0 in range(0, TM_IN, NORM_ROWS):
            rows = slice(r0, r0 + NORM_ROWS)
            h_scr[rows, :] = (_rms(x_ref[rows, :]) * gain + shift).astype(BF16)
        gt_ref[...] = jnp.dot(h_scr[...], wg_ref[...], preferred_element_type=F32)

    z = jnp.dot(h_scr[...], w_ref[...], preferred_element_type=F32)
    q_tile0 = (2 * RG_W) // TN_IN
    is_qk = jnp.logical_and(j >= q_tile0, j < q_tile0 + (2 * DA_W) // TN_IN)

    @pl.when(is_qk)
    def _():
        c = cos_ref[...]
        s = sin_ref[...]
        lane = lax.broadcasted_iota(jnp.int32, (TM_IN, LANE), 1)
        first = (lane % (DA_DK // 2)) < (DA_DK // 4)
        for k in range(TN_IN // LANE):
            zk = z[:, k * LANE:(k + 1) * LANE]
            partner = jnp.where(first, pltpu.roll(zk, LANE - DA_DK // 4, 1), pltpu.roll(zk, DA_DK // 4, 1))
            z_ref[:, k * LANE:(k + 1) * LANE] = zk * c + partner * s

    @pl.when(jnp.logical_not(is_qk))
    def _():
        z_ref[...] = z


def in_projection(x, mod, norm_g, w_in_bf, w_gate_bf, rope_cos, rope_sin, layer, geo):
    n_i = geo.t // TM_IN
    n_j = N_MAIN // TN_IN
    n_ctx_blk = geo.t_ctx // TM_IN
    lat_blk = geo.s_lat // TM_IN

    def rope_idx(i, j):
        return (jnp.where(i < n_ctx_blk, 0, 1 + (i - n_ctx_blk) % lat_blk), 0)

    return pl.pallas_call(
        functools.partial(_inproj_kernel, geo=geo),
        grid=(n_i, n_j),
        in_specs=[pl.BlockSpec((TM_IN, D), lambda i, j: (i, 0)),
                  pl.BlockSpec((None, geo.cond_rows, 6 * D), lambda i, j: (layer, 0, 0)),
                  pl.BlockSpec((None, 1, D), lambda i, j: (layer, 0, 0)),
                  pl.BlockSpec((None, D, TN_IN), lambda i, j: (layer, 0, j)),
                  pl.BlockSpec((None, D, N_GATE), lambda i, j: (layer, 0, 0)),
                  pl.BlockSpec((TM_IN, LANE), rope_idx),
                  pl.BlockSpec((TM_IN, LANE), rope_idx)],
        out_specs=[pl.BlockSpec((TM_IN, TN_IN), lambda i, j: (i, j)),
                   pl.BlockSpec((TM_IN, N_GATE), lambda i, j: (i, 0))],
        out_shape=[jax.ShapeDtypeStruct((geo.t, N_MAIN), F32),
                   jax.ShapeDtypeStruct((geo.t, N_GATE), F32)],
        scratch_shapes=[pltpu.VMEM((TM_IN, D), BF16)],
        compiler_params=_cparams(("arbitrary", "arbitrary")),
        name="in_projection",
    )(x, mod, norm_g, w_in_bf, w_gate_bf, rope_cos, rope_sin)


def rope_tables(s_lat):
    t = jnp.arange(s_lat)
    row = (t // GRID_W).astype(F32)
    col = (t % GRID_W).astype(F32)
    d = jnp.arange(LANE) % DA_DK
    axis = d // (DA_DK // 2)
    n_freq = DA_DK // 4
    inv = ROPE_THETA ** (-(d % n_freq).astype(F32) / n_freq)
    second = (d % (DA_DK // 2)) >= n_freq
    pos = jnp.where(axis[None, :] == 0, row[:, None], col[:, None])
    ang = pos * inv[None, :]
    cos = jnp.cos(ang)
    sin = jnp.sin(ang) * jnp.where(second, 1.0, -1.0)[None, :]
    cos = jnp.concatenate([jnp.ones((TM_IN, LANE), F32), cos], axis=0)
    sin = jnp.concatenate([jnp.zeros((TM_IN, LANE), F32), sin], axis=0)
    return cos, sin


def _chunk_info(g, geo):
    ncc = geo.t_ctx // CHUNK
    cps_c = geo.s_ctx // CHUNK
    cps_l = geo.s_lat // CHUNK
    is_ctx = g < ncc
    sid = jnp.where(is_ctx, g // cps_c, geo.n_ctx + (g - ncc) // cps_l)
    pos = jnp.where(is_ctx, g % cps_c, (g - ncc) % cps_l)
    last = jnp.where(is_ctx, cps_c - 1, cps_l - 1)
    return sid, pos == 0, pos == last


def _rglru_kernel(*refs, geo, rev, n_chunk):
    if rev:
        (xp_ref, x_ref, xn_ref, cw_ref, cb_ref, wg_ref, bg_ref, lam_ref, h0_ref,
         h_ref, st_ref, pad_scr, a_scr, u_scr, carry_scr) = refs
    else:
        (xp_ref, x_ref, xn_ref, y_ref, hb_ref, cw_ref, cb_ref, wg_ref, bg_ref, lam_ref, h0_ref,
         h_ref, st_ref, pad_scr, a_scr, u_scr, carry_scr) = refs
    p = pl.program_id(0)
    g = n_chunk - 1 - p if rev else p
    _, t_first, t_last = _chunk_info(g, geo)
    starts = t_last if rev else t_first

    pad_scr[0:SUBLANE, :] = jnp.where(t_first, 0.0, xp_ref[...])
    pad_scr[SUBLANE:SUBLANE + CHUNK, :] = x_ref[...]
    pad_scr[SUBLANE + CHUNK:2 * SUBLANE + CHUNK, :] = jnp.where(t_last, 0.0, xn_ref[...])
    xc = cb_ref[...]
    for tap in range(4):
        xc = xc + cw_ref[tap:tap + 1, :] * pad_scr[pl.ds(SUBLANE - 2 + tap, CHUNK), :]

    gates = jnp.dot(xc.astype(BF16), wg_ref[...], preferred_element_type=F32) + bg_ref[...]
    r = jax.nn.sigmoid(gates[:, :RG_W])
    i = jax.nn.sigmoid(gates[:, RG_W:])
    log_a = (RG_C * _log_sigmoid(lam_ref[...])) * r
    a = jnp.exp(log_a)
    u = jnp.sqrt(-jnp.tanh(log_a) * (a * a + 1.0)) * (i * xc)

    row = lax.broadcasted_iota(jnp.int32, (CHUNK, RG_W), 0) % SUBLANE
    s = 1
    while s < SUBLANE:
        shift = CHUNK - s if rev else s
        valid = (row < SUBLANE - s) if rev else (row >= s)
        a_sh = pltpu.roll(a, shift, 0)
        u_sh = pltpu.roll(u, shift, 0)
        u = jnp.where(valid, u + a * u_sh, u)
        a = jnp.where(valid, a * a_sh, a)
        s *= 2
    a_scr[...] = a
    u_scr[...] = u

    @pl.when(starts)
    def _():
        carry_scr[...] = h0_ref[...]

    n_grp = CHUNK // SUBLANE

    def body(k, carry):
        grp = n_grp - 1 - k if rev else k
        rows = pl.ds(pl.multiple_of(grp * SUBLANE, SUBLANE), SUBLANE)
        h8 = u_scr[rows, :] + a_scr[rows, :] * carry
        u_scr[rows, :] = h8
        return h8[0:1, :] if rev else h8[SUBLANE - 1:SUBLANE, :]

    carry = lax.fori_loop(0, n_grp, body, carry_scr[...])
    carry_scr[...] = carry
    st_ref[...] = carry
    if rev:
        h_ref[...] = u_scr[...]
    else:
        h_ref[...] = (jax.nn.gelu(y_ref[...]) * (u_scr[...] + hb_ref[...])).astype(BF16)


def rglru_direction(z, hb, conv_w, conv_b, gate_w_bf, gate_b, lam, h0, layer, geo, rev):
    n_chunk = geo.t // CHUNK
    d = 1 if rev else 0
    cpb = CHUNK // SUBLANE
    n_blk8 = geo.t // SUBLANE

    def gi(p):
        return n_chunk - 1 - p if rev else p

    def sid(p):
        return _chunk_info(gi(p), geo)[0]

    x_specs = [pl.BlockSpec((SUBLANE, RG_W), lambda p: (jnp.maximum(gi(p) * cpb - 1, 0), 0)),
               pl.BlockSpec((CHUNK, RG_W), lambda p: (gi(p), 0)),
               pl.BlockSpec((SUBLANE, RG_W), lambda p: (jnp.minimum((gi(p) + 1) * cpb, n_blk8 - 1), 0))]
    w_specs = [pl.BlockSpec((None, 4, RG_W), lambda p: (layer, 0, 0)),
               pl.BlockSpec((None, 1, RG_W), lambda p: (layer, 0, 0)),
               pl.BlockSpec((None, None, RG_W, 2 * RG_W), lambda p: (layer, d, 0, 0)),
               pl.BlockSpec((None, None, 1, 2 * RG_W), lambda p: (layer, d, 0, 0)),
               pl.BlockSpec((None, None, 1, RG_W), lambda p: (layer, d, 0, 0)),
               pl.BlockSpec((None, None, 1, RG_W), lambda p: (sid(p), d, 0, 0))]
    if rev:
        in_specs = x_specs + w_specs
        args = (z, z, z, conv_w, conv_b, gate_w_bf, gate_b, lam, h0)
        h_dtype = F32
    else:
        in_specs = x_specs + [pl.BlockSpec((CHUNK, RG_W), lambda p: (gi(p), 1)),
                              pl.BlockSpec((CHUNK, RG_W), lambda p: (gi(p), 0))] + w_specs
        args = (z, z, z, z, hb, conv_w, conv_b, gate_w_bf, gate_b, lam, h0)
        h_dtype = BF16
    return pl.pallas_call(
        functools.partial(_rglru_kernel, geo=geo, rev=rev, n_chunk=n_chunk),
        grid=(n_chunk,),
        in_specs=in_specs,
        out_specs=[pl.BlockSpec((CHUNK, RG_W), lambda p: (gi(p), 0)),
                   pl.BlockSpec((None, 1, RG_W), lambda p: (sid(p), 0, 0))],
        out_shape=[jax.ShapeDtypeStruct((geo.t, RG_W), h_dtype),
                   jax.ShapeDtypeStruct((geo.n_seq, 1, RG_W), F32)],
        scratch_shapes=[pltpu.VMEM((CHUNK + 2 * SUBLANE, RG_W), F32),
                        pltpu.VMEM((CHUNK, RG_W), F32),
                        pltpu.VMEM((CHUNK, RG_W), F32),
                        pltpu.VMEM((1, RG_W), F32)],
        compiler_params=_cparams(("arbitrary",)),
        name="rglru_bwd" if rev else "rglru_fwd",
    )(*args)


def _attn_kernel(*refs, past, n_new, kv_chunk, lam_init):
    if past:
        q_ref, kc_ref, vc_ref, k_ref, v_ref, lam_ref, g_ref, o_ref, kt_scr, v_scr = refs
    else:
        q_ref, k_ref, v_ref, lam_ref, g_ref, o_ref, kt_scr, v_scr = refs
    n_kv = past + n_new
    step = min(KV_STEP, n_new)

    @pl.when(pl.program_id(2) == 0)
    def _():
        for r0 in range(0, past, step):
            rows = slice(r0, r0 + step)
            kt_scr[:, rows] = kc_ref[rows, :].T.astype(BF16)
            v_scr[rows, 0:DA_DV] = vc_ref[rows, :].astype(BF16)
        for r0 in range(0, n_new, step):
            src = slice(r0, r0 + step)
            dst = slice(past + r0, past + r0 + step)
            kt_scr[:, dst] = k_ref[src, :].T.astype(BF16)
            v_scr[dst, 0:DA_DV] = v_ref[src, :].astype(BF16)
        v_scr[:, DA_DV:2 * DA_DV] = jnp.ones((n_kv, DA_DV), BF16)

    q = q_ref[...] * (DA_DK ** -0.5 * LOG2E)
    lane = lax.broadcasted_iota(jnp.int32, q.shape, 1)
    heads = []
    for c in range(2):
        in_comp = (lane < DA_DK) if c == 0 else (lane >= DA_DK)
        qc = jnp.where(in_comp, q, 0.0).astype(BF16)
        m = None
        acc = None
        for c0 in range(0, n_kv, kv_chunk):
            cols = slice(c0, c0 + kv_chunk)
            s = jnp.dot(qc, kt_scr[:, cols], preferred_element_type=F32)
            mx = jnp.max(s, axis=-1, keepdims=True)
            m_new = mx if m is None else jnp.maximum(m, mx)
            pv = jnp.dot(jnp.exp2(s - m_new).astype(BF16), v_scr[cols, :], preferred_element_type=F32)
            acc = pv if m is None else jnp.exp2(m - m_new) * acc + pv
            m = m_new
        heads.append(acc[:, 0:DA_DV] / acc[:, DA_DV:2 * DA_DV])

    lq = lam_ref[...]
    lam = (jnp.exp(jnp.sum(lq[0:1] * lq[1:2], axis=-1, keepdims=True))
           - jnp.exp(jnp.sum(lq[2:3] * lq[3:4], axis=-1, keepdims=True)) + lam_init)
    o = heads[0] - lam * heads[1]
    o_ref[...] = (_rms(o) * g_ref[...] * (1.0 - lam_init)).astype(BF16)


def _kv_chunk(n_kv):
    for cand in KV_CHUNKS:
        if n_kv % cand == 0:
            return cand
    return n_kv


def _attn_scratch(n_kv):
    return [pltpu.VMEM((2 * DA_DK, n_kv), BF16), pltpu.VMEM((n_kv, 2 * DA_DV), BF16)]


_Q_COL = (2 * RG_W) // LANE
_K_COL = _Q_COL + DA_W // LANE
_V_COL = _K_COL + DA_W // LANE


def attention_context(z, da_lambda, da_norm_g, layer, lam_init, geo):
    s = geo.s_ctx
    return pl.pallas_call(
        functools.partial(_attn_kernel, past=0, n_new=s, kv_chunk=_kv_chunk(s), lam_init=lam_init),
        grid=(geo.n_ctx, DA_H, 1),
        in_specs=[pl.BlockSpec((s, LANE), lambda b, h, qi: (b, _Q_COL + h)),
                  pl.BlockSpec((s, LANE), lambda b, h, qi: (b, _K_COL + h)),
                  pl.BlockSpec((s, LANE), lambda b, h, qi: (b, _V_COL + h)),
                  pl.BlockSpec((None, 4, DA_DK), lambda b, h, qi: (layer, 0, 0)),
                  pl.BlockSpec((None, 1, DA_DV), lambda b, h, qi: (layer, 0, 0))],
        out_specs=pl.BlockSpec((s, LANE), lambda b, h, qi: (b, h)),
        out_shape=jax.ShapeDtypeStruct((geo.t_ctx, DA_W), BF16),
        scratch_shapes=_attn_scratch(s),
        compiler_params=_cparams(("arbitrary",) * 3),
        name="attention_context",
    )(z, z, z, da_lambda, da_norm_g)


def attention_latent(z, cache_k, cache_v, da_lambda, da_norm_g, layer, lam_init, geo):
    past = cache_k.shape[2]
    s = geo.s_lat
    n_kv = past + s
    nq = s // TQ
    q0 = geo.t_ctx // TQ
    k0 = geo.t_ctx // s
    return pl.pallas_call(
        functools.partial(_attn_kernel, past=past, n_new=s, kv_chunk=_kv_chunk(n_kv), lam_init=lam_init),
        grid=(geo.n_lat, DA_H, nq),
        in_specs=[pl.BlockSpec((TQ, LANE), lambda b, h, qi: (q0 + b * nq + qi, _Q_COL + h)),
                  pl.BlockSpec((None, None, past, LANE), lambda b, h, qi: (b, layer, 0, h)),
                  pl.BlockSpec((None, None, past, LANE), lambda b, h, qi: (b, layer, 0, h)),
                  pl.BlockSpec((s, LANE), lambda b, h, qi: (k0 + b, _K_COL + h)),
                  pl.BlockSpec((s, LANE), lambda b, h, qi: (k0 + b, _V_COL + h)),
                  pl.BlockSpec((None, 4, DA_DK), lambda b, h, qi: (layer, 0, 0)),
                  pl.BlockSpec((None, 1, DA_DV), lambda b, h, qi: (layer, 0, 0))],
        out_specs=pl.BlockSpec((TQ, LANE), lambda b, h, qi: (b * nq + qi, h)),
        out_shape=jax.ShapeDtypeStruct((geo.t_lat, DA_W), BF16),
        scratch_shapes=_attn_scratch(n_kv),
        compiler_params=_cparams(("arbitrary",) * 3),
        name="attention_latent",
    )(z, cache_k, cache_v, z, z, da_lambda, da_norm_g)


def _mlstm_kernel(*refs, geo, rev, n_chunk):
    if rev:
        (q_ref, k_ref, v_ref, gt_ref, gb_ref, c0_ref, n0_ref, m0_ref,
         h_ref, co_ref, no_ref, mo_ref, c_scr, n_scr, m_scr) = refs
    else:
        (q_ref, k_ref, v_ref, o_ref, hb_ref, gt_ref, gb_ref, ng_ref, c0_ref, n0_ref, m0_ref,
         h_ref, co_ref, no_ref, mo_ref, c_scr, n_scr, m_scr) = refs
    p = pl.program_id(0)
    g = n_chunk - 1 - p if rev else p
    _, t_first, t_last = _chunk_info(g, geo)
    starts = t_last if rev else t_first
    d = 1 if rev else 0

    @pl.when(starts)
    def _():
        c_scr[...] = c0_ref[...]
        n_scr[...] = n0_ref[...]
        m_scr[...] = m0_ref[...]

    gt = gt_ref[...] + gb_ref[...]
    eye = jnp.where(lax.broadcasted_iota(jnp.int32, (N_GATE, N_GATE), 0)
                    == lax.broadcasted_iota(jnp.int32, (N_GATE, N_GATE), 1), 1.0, 0.0)
    gtt = lax.dot_general(eye, gt, (((1,), (1,)), ((), ())), precision=HIGHEST, preferred_element_type=F32)
    ti = lax.broadcasted_iota(jnp.int32, (CHUNK, CHUNK), 0)
    si = lax.broadcasted_iota(jnp.int32, (CHUNK, CHUNK), 1)
    keep = (si >= ti) if rev else (si <= ti)
    tri = jnp.where(keep, 1.0, 0.0)
    b_col_all = jnp.dot(tri, _log_sigmoid(gt), precision=HIGHEST, preferred_element_type=F32)
    tri_t = jnp.where((ti >= si) if rev else (ti <= si), 1.0, 0.0)
    b_row_all = jnp.dot(_log_sigmoid(gtt), tri_t, precision=HIGHEST, preferred_element_type=F32)
    edge = 0 if rev else CHUNK - 1

    for h in range(ML_H):
        ci = d * 2 * ML_H + h
        cf = ci + ML_H
        hs = slice(h * ML_DH, (h + 1) * ML_DH)
        qh = q_ref[:, hs]
        kh = k_ref[:, hs] * (ML_DH ** -0.5)
        vh = v_ref[:, hs]
        qb = qh.astype(BF16)
        kb = kh.astype(BF16)
        vb = vh.astype(BF16)
        li_col = gt[:, ci:ci + 1]
        li_row = gtt[ci:ci + 1, :]
        b_col = b_col_all[:, cf:cf + 1]
        b_row = b_row_all[cf:cf + 1, :]
        m_prev = m_scr[h][:, 0:1]
        c_prev = c_scr[h]
        n_prev = n_scr[h]

        dm = jnp.where(keep, b_col - b_row + li_row, -jnp.inf)
        inter = b_col + m_prev
        m_t = jnp.maximum(inter, jnp.max(dm, axis=-1, keepdims=True))
        s = lax.dot_general(qb, kb, (((1,), (1,)), ((), ())), preferred_element_type=F32) * jnp.exp(dm - m_t)
        w_inter = jnp.exp(inter - m_t)
        cq = lax.dot_general(qb, c_prev.astype(BF16), (((1,), (1,)), ((), ())), preferred_element_type=F32)
        num = jnp.dot(s.astype(BF16), vb, preferred_element_type=F32) + w_inter * cq
        den = jnp.sum(s, axis=-1, keepdims=True) + w_inter * jnp.sum(qh * n_prev, axis=-1, keepdims=True)
        hh = num / jnp.maximum(jnp.abs(den), jnp.exp(-m_t))

        b_tot = b_col[edge:edge + 1, :]
        gg = b_tot - b_col + li_col
        m_new = jnp.maximum(b_tot + m_prev, jnp.max(gg, axis=0, keepdims=True))
        wgt = jnp.exp(gg - m_new)
        decay = jnp.exp(b_tot + m_prev - m_new)
        wv_t = (wgt * vh).T.astype(BF16)
        c_new = decay * c_prev + jnp.dot(wv_t, kb, preferred_element_type=F32)
        n_new = decay * n_prev + jnp.sum(wgt * kh, axis=0, keepdims=True)
        c_scr[h] = c_new
        n_scr[h] = n_new
        m_scr[h] = jnp.broadcast_to(m_new, (1, ML_DH))
        co_ref[h] = c_new
        no_ref[h] = n_new
        mo_ref[h] = jnp.broadcast_to(m_new, (1, ML_DH))

        if rev:
            h_ref[:, hs] = hh
        else:
            gated = jax.nn.sigmoid(o_ref[:, hs]) * (hh + hb_ref[:, hs])
            h_ref[:, hs] = (_rms(gated) * ng_ref[:, hs]).astype(BF16)


_MLQ_COL = (2 * RG_W + 3 * DA_W) // ML_W


def mlstm_direction(z, hb, gates, gate_b, norm_g, c0, n0, m0, layer, geo, rev):
    n_chunk = geo.t // CHUNK
    d = 1 if rev else 0

    def gi(p):
        return n_chunk - 1 - p if rev else p

    def sid(p):
        return _chunk_info(gi(p), geo)[0]

    def zspec(col):
        return pl.BlockSpec((CHUNK, ML_W), lambda p: (gi(p), col))

    qkv = [zspec(_MLQ_COL), zspec(_MLQ_COL + 1), zspec(_MLQ_COL + 2)]
    gate_specs = [pl.BlockSpec((CHUNK, N_GATE), lambda p: (gi(p), 0)),
                  pl.BlockSpec((None, 1, N_GATE), lambda p: (layer, 0, 0))]
    st_specs = [pl.BlockSpec((None, None, ML_H, ML_DH, ML_DH), lambda p: (sid(p), d, 0, 0, 0)),
                pl.BlockSpec((None, None, ML_H, 1, ML_DH), lambda p: (sid(p), d, 0, 0, 0)),
                pl.BlockSpec((None, None, ML_H, 1, ML_DH), lambda p: (sid(p), d, 0, 0, 0))]
    if rev:
        in_specs = qkv + gate_specs + st_specs
        args = (z, z, z, gates, gate_b, c0, n0, m0)
        h_dtype = F32
    else:
        in_specs = (qkv + [zspec(_MLQ_COL + 3), pl.BlockSpec((CHUNK, ML_W), lambda p: (gi(p), 0))] + gate_specs
                    + [pl.BlockSpec((None, 1, ML_W), lambda p: (layer, 0, 0))] + st_specs)
        args = (z, z, z, z, hb, gates, gate_b, norm_g, c0, n0, m0)
        h_dtype = BF16
    return pl.pallas_call(
        functools.partial(_mlstm_kernel, geo=geo, rev=rev, n_chunk=n_chunk),
        grid=(n_chunk,),
        in_specs=in_specs,
        out_specs=[pl.BlockSpec((CHUNK, ML_W), lambda p: (gi(p), 0)),
                   pl.BlockSpec((None, ML_H, ML_DH, ML_DH), lambda p: (sid(p), 0, 0, 0)),
                   pl.BlockSpec((None, ML_H, 1, ML_DH), lambda p: (sid(p), 0, 0, 0)),
                   pl.BlockSpec((None, ML_H, 1, ML_DH), lambda p: (sid(p), 0, 0, 0))],
        out_shape=[jax.ShapeDtypeStruct((geo.t, ML_W), h_dtype),
                   jax.ShapeDtypeStruct((geo.n_seq, ML_H, ML_DH, ML_DH), F32),
                   jax.ShapeDtypeStruct((geo.n_seq, ML_H, 1, ML_DH), F32),
                   jax.ShapeDtypeStruct((geo.n_seq, ML_H, 1, ML_DH), F32)],
        scratch_shapes=[pltpu.VMEM((ML_H, ML_DH, ML_DH), F32),
                        pltpu.VMEM((ML_H, 1, ML_DH), F32),
                        pltpu.VMEM((ML_H, 1, ML_DH), F32)],
        compiler_params=_cparams(("arbitrary",)),
        name="mlstm_bwd" if rev else "mlstm_fwd",
    )(*args)


def _outproj_kernel(x_ref, rg_ref, dac_ref, dal_ref, ml_ref, w_ref, mod_ref, g_ref, wr_ref, br_ref,
                    x1_ref, h2_ref, te_ref, tw_ref, *, geo):
    i = pl.program_id(0)
    r = _cond_row(i, geo, TM_OUT)
    gate1 = mod_ref[pl.ds(r, 1), 2 * D:3 * D]
    shift2 = mod_ref[pl.ds(r, 1), 3 * D:4 * D]
    scale2 = mod_ref[pl.ds(r, 1), 4 * D:5 * D]
    da = jnp.where(i < geo.t_ctx // TM_OUT, dac_ref[...], dal_ref[...])
    y = jnp.dot(rg_ref[...], w_ref[0:RG_W, :], preferred_element_type=F32)
    y = y + jnp.dot(da, w_ref[RG_W:RG_W + DA_W, :], preferred_element_type=F32)
    y = y + jnp.dot(ml_ref[...], w_ref[RG_W + DA_W:D, :], preferred_element_type=F32)
    x1 = x_ref[...] + gate1 * y
    x1_ref[...] = x1
    h2 = _rms(x1) * g_ref[...] * (1.0 + scale2) + shift2
    lo = lax.bitcast_convert_type(h2[:, 0:D // 2].astype(BF16).astype(F32), jnp.uint32) >> 16
    hi = lax.bitcast_convert_type(h2[:, D // 2:D].astype(BF16).astype(F32), jnp.uint32)
    h2_ref[...] = hi | lo

    logits = jnp.dot(h2.astype(BF16), wr_ref[...], preferred_element_type=F32) + br_ref[...]
    lane = lax.broadcasted_iota(jnp.int32, logits.shape, 1)
    lane_f = lane.astype(F32)
    sel_e = jnp.zeros(logits.shape, jnp.int32)
    sel_v = jnp.zeros(logits.shape, F32)
    top0 = None
    total = None
    for k in range(TOP_K):
        mk = jnp.max(logits, axis=-1, keepdims=True)
        idx = jnp.min(jnp.where(logits == mk, lane_f, float(LANE)), axis=-1, keepdims=True)
        if k == 0:
            top0 = mk
        ek = jnp.exp(mk - top0)
        total = ek if total is None else total + ek
        sel_e = jnp.where(lane == k, idx.astype(jnp.int32), sel_e)
        sel_v = jnp.where(lane == k, ek, sel_v)
        logits = jnp.where(lane_f == idx, -jnp.inf, logits)
    te_ref[...] = sel_e
    tw_ref[...] = sel_v / total


def out_projection(x, out_rg, da_ctx, da_lat, out_ml, w_out_bf, mod, norm_g, router_w, router_b, layer, geo):
    n_i = geo.t // TM_OUT
    n_c = geo.t_ctx // TM_OUT
    return pl.pallas_call(
        functools.partial(_outproj_kernel, geo=geo),
        grid=(n_i,),
        in_specs=[pl.BlockSpec((TM_OUT, D), lambda i: (i, 0)),
                  pl.BlockSpec((TM_OUT, RG_W), lambda i: (i, 0)),
                  pl.BlockSpec((TM_OUT, DA_W), lambda i: (jnp.minimum(i, n_c - 1), 0)),
                  pl.BlockSpec((TM_OUT, DA_W), lambda i: (jnp.maximum(i - n_c, 0), 0)),
                  pl.BlockSpec((TM_OUT, ML_W), lambda i: (i, 0)),
                  pl.BlockSpec((None, D, D), lambda i: (layer, 0, 0), pipeline_mode=pl.Buffered(1)),
                  pl.BlockSpec((None, geo.cond_rows, 6 * D), lambda i: (layer, 0, 0), pipeline_mode=pl.Buffered(1)),
                  pl.BlockSpec((None, 1, D), lambda i: (layer, 0, 0)),
                  pl.BlockSpec((None, D, LANE), lambda i: (layer, 0, 0), pipeline_mode=pl.Buffered(1)),
                  pl.BlockSpec((None, 1, LANE), lambda i: (layer, 0, 0))],
        out_specs=[pl.BlockSpec((TM_OUT, D), lambda i: (i, 0)),
                   pl.BlockSpec((TM_OUT, D // 2), lambda i: (i, 0)),
                   pl.BlockSpec((TM_OUT, LANE), lambda i: (i, 0)),
                   pl.BlockSpec((TM_OUT, LANE), lambda i: (i, 0))],
        out_shape=[jax.ShapeDtypeStruct((geo.t, D), F32),
                   jax.ShapeDtypeStruct((geo.t, D // 2), jnp.uint32),
                   jax.ShapeDtypeStruct((geo.t, LANE), jnp.int32),
                   jax.ShapeDtypeStruct((geo.t, LANE), F32)],
        compiler_params=_cparams(("arbitrary",)),
        name="out_projection",
    )(x, out_rg, da_ctx, da_lat, out_ml, w_out_bf, mod, norm_g, router_w, router_b)


def _dispatch_kernel(dest_ref, h_ref, xs_in, xs_out, sem):
    del xs_in

    def body(r, carry):
        for k in range(TOP_K):
            d = dest_ref[r * TOP_K + k]
            pltpu.make_async_copy(h_ref.at[pl.ds(r, 1)], xs_out.at[pl.ds(d, 1)], sem).start()
        return carry

    lax.fori_loop(0, TM_DSP, body, 0)
    for k in range(TOP_K):
        pltpu.make_async_copy(h_ref, xs_out.at[pl.ds(0, TM_DSP)], sem).wait()


def moe_dispatch(dest, h2_packed, n_rows):
    t = h2_packed.shape[0]
    return pl.pallas_call(
        _dispatch_kernel,
        grid=(t // TM_DSP,),
        in_specs=[pl.BlockSpec((TM_DSP * TOP_K,), lambda i: (i,), memory_space=pltpu.SMEM),
                  pl.BlockSpec((TM_DSP, D // 2), lambda i: (i, 0)),
                  pl.BlockSpec(memory_space=pl.ANY)],
        out_specs=pl.BlockSpec(memory_space=pl.ANY),
        out_shape=jax.ShapeDtypeStruct((n_rows, D // 2), jnp.uint32),
        scratch_shapes=[pltpu.SemaphoreType.DMA(())],
        input_output_aliases={2: 0},
        compiler_params=_cparams(("arbitrary",), disable_bounds_checks=True),
        name="moe_dispatch",
    )(dest, h2_packed, jnp.zeros((n_rows, D // 2), jnp.uint32))


def _moe_kernel(be_ref, nu_ref, x_ref, wg_ref, wu_ref, bg_ref, bu_ref, wd_ref, bd_ref, o_ref, x_scr, act_scr, *, n_a):
    m = pl.program_id(0)
    s = pl.program_id(1)
    used = m < nu_ref[0]
    half = D // 2

    @pl.when(jnp.logical_and(used, s == 0))
    def _():
        u = x_ref[...]
        x_scr[:, 0:half] = lax.bitcast_convert_type(u << 16, F32).astype(BF16)
        x_scr[:, half:D] = lax.bitcast_convert_type((u >> 16) << 16, F32).astype(BF16)

    @pl.when(jnp.logical_and(used, s < n_a))
    def _():
        x = x_scr[...]
        gate = jnp.dot(x, wg_ref[...].astype(BF16), preferred_element_type=F32) + bg_ref[...]
        up = jnp.dot(x, wu_ref[...].astype(BF16), preferred_element_type=F32) + bu_ref[...]
        gate = jnp.minimum(gate, SWIGLU_LIMIT)
        up = jnp.clip(up, -SWIGLU_LIMIT, SWIGLU_LIMIT)
        act = (gate * jax.nn.sigmoid(SWIGLU_ALPHA * gate) * (up + 1.0)).astype(BF16)
        for j in range(n_a):
            @pl.when(s == j)
            def _():
                act_scr[:, j * TF_MOE:(j + 1) * TF_MOE] = act

    @pl.when(jnp.logical_and(used, s >= n_a))
    def _():
        o_ref[...] = jnp.dot(act_scr[...], wd_ref[...].astype(BF16), preferred_element_type=F32) + bd_ref[...]

    @pl.when(jnp.logical_and(jnp.logical_not(used), s >= n_a))
    def _():
        o_ref[...] = jnp.zeros(o_ref.shape, F32)


def moe_experts(x_sorted, blk_e, n_used, w_gu, b_gu, w_dn, b_dn, layer):
    n_rows = x_sorted.shape[0]
    n_blk = n_rows // TM_MOE
    n_a = FF // TF_MOE
    n_b = D // TN_MOE

    def ja(m, s, nu):
        return jnp.where(m < nu[0], jnp.minimum(s, n_a - 1), n_a - 1)

    def jb(m, s, nu):
        return jnp.where(m < nu[0], jnp.maximum(s - n_a, 0), n_b - 1)

    grid_spec = pltpu.PrefetchScalarGridSpec(
        num_scalar_prefetch=2,
        grid=(n_blk, n_a + n_b),
        in_specs=[pl.BlockSpec((TM_MOE, D // 2), lambda m, s, be, nu: (m, 0)),
                  pl.BlockSpec((None, None, D, TF_MOE), lambda m, s, be, nu: (layer, be[m], 0, ja(m, s, nu))),
                  pl.BlockSpec((None, None, D, TF_MOE), lambda m, s, be, nu: (layer, be[m], 0, n_a + ja(m, s, nu))),
                  pl.BlockSpec((None, None, 1, TF_MOE), lambda m, s, be, nu: (layer, be[m], 0, ja(m, s, nu))),
                  pl.BlockSpec((None, None, 1, TF_MOE), lambda m, s, be, nu: (layer, be[m], 0, n_a + ja(m, s, nu))),
                  pl.BlockSpec((None, None, FF, TN_MOE), lambda m, s, be, nu: (layer, be[m], 0, jb(m, s, nu))),
                  pl.BlockSpec((None, None, 1, TN_MOE), lambda m, s, be, nu: (layer, be[m], 0, jb(m, s, nu)))],
        out_specs=pl.BlockSpec((TM_MOE, TN_MOE), lambda m, s, be, nu: (m, jnp.maximum(s - n_a, 0))),
        scratch_shapes=[pltpu.VMEM((TM_MOE, D), BF16), pltpu.VMEM((TM_MOE, FF), BF16)],
    )
    n_layer, n_exp = w_gu.shape[:2]
    return pl.pallas_call(
        functools.partial(_moe_kernel, n_a=n_a),
        grid_spec=grid_spec,
        out_shape=jax.ShapeDtypeStruct((n_rows, D), F32),
        compiler_params=_cparams(("arbitrary", "arbitrary")),
        name="moe_experts",
    )(blk_e, n_used, x_sorted, w_gu, w_gu, b_gu.reshape(n_layer, n_exp, 1, 2 * FF),
      b_gu.reshape(n_layer, n_exp, 1, 2 * FF), w_dn, b_dn.reshape(n_layer, n_exp, 1, D))


def moe_routing(top_e):
    flat_e = top_e.reshape(-1)
    n_assign = flat_e.shape[0]
    onehot = (flat_e[:, None] == jnp.arange(N_EXP, dtype=jnp.int32)[None, :]).astype(jnp.int32)
    csum = jnp.cumsum(onehot, axis=0)
    counts = csum[-1]
    pos = jnp.take_along_axis(csum, flat_e[:, None], axis=1)[:, 0] - 1
    padded = (counts + TM_MOE - 1) // TM_MOE * TM_MOE
    pad_end = jnp.cumsum(padded)
    pad_start = pad_end - padded
    dest = pad_start[flat_e] + pos
    n_blk = moe_num_blocks(n_assign)
    blk_e = jnp.minimum(jnp.searchsorted(pad_end, jnp.arange(n_blk, dtype=jnp.int32) * TM_MOE, side='right'),
                        N_EXP - 1).astype(jnp.int32)
    n_used = (pad_end[-1] // TM_MOE).astype(jnp.int32).reshape(1)
    return dest.astype(jnp.int32), blk_e, n_used


def moe_num_blocks(n_assign):
    return -(-n_assign // TM_MOE) + N_EXP


def _combine_kernel(dfirst_ref, dnext_ref, x_ref, tw_ref, mod_ref, yb_ref, o_ref, buf, sem, *, geo, n_blk):
    i = pl.program_id(0)

    def gather(d_ref, slot):
        def body(r, carry):
            for k in range(TOP_K):
                d = d_ref[r * TOP_K + k]
                pltpu.make_async_copy(yb_ref.at[pl.ds(d, 1)], buf.at[slot, k, pl.ds(r, 1)], sem.at[slot]).start()
            return carry

        lax.fori_loop(0, TM_CMB, body, 0)

    @pl.when(i == 0)
    def _():
        gather(dfirst_ref, 0)

    @pl.when(i + 1 < n_blk)
    def _():
        gather(dnext_ref, (i + 1) % 2)

    slot = i % 2
    for k in range(TOP_K):
        pltpu.make_async_copy(yb_ref.at[pl.ds(0, TM_CMB)], buf.at[slot, k], sem.at[slot]).wait()

    r = _cond_row(i, geo, TM_CMB)
    gate2 = mod_ref[pl.ds(r, 1), 5 * D:6 * D]
    tw = tw_ref[...]
    y = tw[:, 0:1] * buf[slot, 0]
    for k in range(1, TOP_K):
        y = y + tw[:, k:k + 1] * buf[slot, k]
    o_ref[...] = x_ref[...] + gate2 * y


def moe_combine(dest, x1, top_w, yb, mod, layer, geo):
    n_blk = geo.t // TM_CMB
    dblk = TM_CMB * TOP_K
    return pl.pallas_call(
        functools.partial(_combine_kernel, geo=geo, n_blk=n_blk),
        grid=(n_blk,),
        in_specs=[pl.BlockSpec((dblk,), lambda i: (0,), memory_space=pltpu.SMEM),
                  pl.BlockSpec((dblk,), lambda i: (jnp.minimum(i + 1, n_blk - 1),), memory_space=pltpu.SMEM),
                  pl.BlockSpec((TM_CMB, D), lambda i: (i, 0)),
                  pl.BlockSpec((TM_CMB, LANE), lambda i: (i, 0)),
                  pl.BlockSpec((None, geo.cond_rows, 6 * D), lambda i: (layer, 0, 0)),
                  pl.BlockSpec(memory_space=pl.ANY)],
        out_specs=pl.BlockSpec((TM_CMB, D), lambda i: (i, 0)),
        out_shape=jax.ShapeDtypeStruct((geo.t, D), F32),
        scratch_shapes=[pltpu.VMEM((2, TOP_K, TM_CMB, D), F32), pltpu.SemaphoreType.DMA((2,))],
        compiler_params=_cparams(("arbitrary",), disable_bounds_checks=True),
        name="moe_combine",
    )(dest, dest, x1, top_w, mod, yb)


def _final_norm_kernel(x_ref, g_ref, o_ref):
    o_ref[...] = _rms(x_ref[...]) * g_ref[...]


def final_norm(x, g):
    t = x.shape[0]
    return pl.pallas_call(
        _final_norm_kernel,
        grid=(t // TM_EW,),
        in_specs=[pl.BlockSpec((TM_EW, D), lambda i: (i, 0)), pl.BlockSpec((1, D), lambda i: (0, 0))],
        out_specs=pl.BlockSpec((TM_EW, D), lambda i: (i, 0)),
        out_shape=jax.ShapeDtypeStruct((t, D), F32),
        compiler_params=_cparams(("arbitrary",)),
        name="final_norm",
    )(x, g.reshape(1, D))


def _block_diag(w):
    eye = jnp.eye(RG_BLOCKS, dtype=w.dtype)
    dense = w[..., :, :, None, :] * eye[:, None, :, None]
    return dense.reshape(*w.shape[:-3], RG_W, RG_W)


def kernel(x_prompt, x_sample, c, cache_k, cache_v, state_rglru, state_mlstm_C, state_mlstm_n, state_mlstm_m, c_ctx, ada_w, ada_b, norm1_g, norm2_g, w_in, rg_conv_w, rg_conv_b, rg_gate_w, rg_gate_b, rg_lambda, da_lambda, da_norm_g, ml_gate_b, ml_norm_g, w_out, router_w, router_b, moe_w_gu, moe_b_gu, moe_w_down, moe_b_down, final_g):
    n_ctx, s_ctx, _ = x_prompt.shape
    n_lat, s_lat, _ = x_sample.shape
    n_layer = ada_w.shape[0]
    past = cache_k.shape[2]
    geo = Geo(n_ctx, s_ctx, n_lat, s_lat)
    assert geo.t_ctx % TM_IN == 0 and s_lat % TM_IN == 0 and s_ctx % CHUNK == 0 and s_lat % CHUNK == 0
    assert s_lat % TQ == 0 and geo.t_ctx % s_lat == 0 and past % KV_STEP == 0 and s_ctx % SUBLANE == 0

    x = jnp.concatenate([x_prompt.reshape(geo.t_ctx, D), x_sample.reshape(geo.t_lat, D)], axis=0)
    cond = jnp.zeros((geo.cond_rows, D), F32).at[:n_lat].set(c).at[n_lat].set(c_ctx)
    mod = ada_modulation(cond, ada_w, ada_b)

    w_in_bf = w_in.astype(BF16)
    w_gate_bf = w_in_bf[:, :, N_MAIN:]
    w_out_bf = w_out.astype(BF16)
    rg_gate_dense = jnp.concatenate([_block_diag(rg_gate_w[:, :, 0]), _block_diag(rg_gate_w[:, :, 1])],
                                    axis=-1).astype(BF16)
    rg_gate_bias = rg_gate_b.reshape(n_layer, 2, 1, 2 * RG_W)
    rg_lam = rg_lambda.reshape(n_layer, 2, 1, RG_W)
    rg_cb = rg_conv_b.reshape(n_layer, 1, RG_W)
    ml_gb = ml_gate_b.reshape(n_layer, 1, N_GATE)
    ml_ng = ml_norm_g.reshape(n_layer, 1, ML_W)
    da_ng = da_norm_g.reshape(n_layer, 1, DA_DV)
    n1g = norm1_g.reshape(n_layer, 1, D)
    n2g = norm2_g.reshape(n_layer, 1, D)
    router_w_pad = jnp.pad(router_w, ((0, 0), (0, 0), (0, LANE - N_EXP))).astype(BF16)
    router_b_pad = jnp.pad(router_b, ((0, 0), (0, LANE - N_EXP)), constant_values=-1e30).reshape(n_layer, 1, LANE)
    rope_cos, rope_sin = rope_tables(s_lat)
    cache_k4 = cache_k.reshape(n_lat, n_layer, past, DA_W)
    cache_v4 = cache_v.reshape(n_lat, n_layer, past, DA_W)

    ks, vs, rgs, cs, ns, ms = [], [], [], [], [], []
    for l in range(n_layer):
        lam_init = 0.8 - 0.6 * math.exp(-0.3 * l)
        z, gates = in_projection(x, mod, n1g, w_in_bf, w_gate_bf, rope_cos, rope_sin, l, geo)

        rg_h0 = jnp.concatenate([jnp.zeros((n_ctx, 2, RG_W), F32), state_rglru[:, l]], axis=0).reshape(geo.n_seq, 2, 1, RG_W)
        hb, rg_sb = rglru_direction(z, None, rg_conv_w, rg_cb, rg_gate_dense, rg_gate_bias, rg_lam, rg_h0, l, geo, True)
        out_rg, rg_sf = rglru_direction(z, hb, rg_conv_w, rg_cb, rg_gate_dense, rg_gate_bias, rg_lam, rg_h0, l, geo, False)

        da_ctx = attention_context(z, da_lambda, da_ng, l, lam_init, geo)
        da_lat = attention_latent(z, cache_k4, cache_v4, da_lambda, da_ng, l, lam_init, geo)

        c0 = jnp.concatenate([jnp.zeros((n_ctx, 2, ML_H, ML_DH, ML_DH), F32), state_mlstm_C[:, l]], axis=0)
        n0 = jnp.concatenate([jnp.zeros((n_ctx, 2, ML_H, ML_DH), F32), state_mlstm_n[:, l]], axis=0)
        n0 = n0.reshape(geo.n_seq, 2, ML_H, 1, ML_DH)
        m0 = jnp.concatenate([jnp.zeros((n_ctx, 2, ML_H), F32), state_mlstm_m[:, l]], axis=0)
        m0 = jnp.broadcast_to(m0[..., None, None], (geo.n_seq, 2, ML_H, 1, ML_DH))
        mhb, cb, nb, mb = mlstm_direction(z, None, gates, ml_gb, ml_ng, c0, n0, m0, l, geo, True)
        out_ml, cf, nf, mf = mlstm_direction(z, mhb, gates, ml_gb, ml_ng, c0, n0, m0, l, geo, False)

        x1, h2_packed, top_e, top_w = out_projection(x, out_rg, da_ctx, da_lat, out_ml, w_out_bf, mod, n2g,
                                                     router_w_pad, router_b_pad, l, geo)
        dest, blk_e, n_used = moe_routing(top_e[:, :TOP_K])
        x_sorted = moe_dispatch(dest, h2_packed, moe_num_blocks(geo.t * TOP_K) * TM_MOE)
        yb = moe_experts(x_sorted, blk_e, n_used, moe_w_gu, moe_b_gu, moe_w_down, moe_b_down, l)
        x = moe_combine(dest, x1, top_w, yb, mod, l, geo)

        ks.append(z[:geo.t_ctx, 2 * RG_W + DA_W:2 * RG_W + 2 * DA_W].reshape(n_ctx, s_ctx, DA_H, DA_DV))
        vs.append(z[:geo.t_ctx, 2 * RG_W + 2 * DA_W:2 * RG_W + 3 * DA_W].reshape(n_ctx, s_ctx, DA_H, DA_DV))
        rgs.append(jnp.stack([rg_sf[:n_ctx, 0], rg_sb[:n_ctx, 0]], axis=1))
        cs.append(jnp.stack([cf[:n_ctx], cb[:n_ctx]], axis=1))
        ns.append(jnp.stack([nf[:n_ctx, :, 0], nb[:n_ctx, :, 0]], axis=1))
        ms.append(jnp.stack([mf[:n_ctx, :, 0, 0], mb[:n_ctx, :, 0, 0]], axis=1))

    y = final_norm(x, final_g)
    y_prompt = y[:geo.t_ctx].reshape(n_ctx, s_ctx, D)
    y_sample = y[geo.t_ctx:].reshape(n_lat, s_lat, D)
    return (y_prompt, y_sample, jnp.stack(ks, axis=1), jnp.stack(vs, axis=1), jnp.stack(rgs, axis=1),
            jnp.stack(cs, axis=1), jnp.stack(ns, axis=1), jnp.stack(ms, axis=1))
```

```python
import functools
import math

import jax
import jax.numpy as jnp
from jax import lax
from jax.experimental import pallas as pl
from jax.experimental.pallas import tpu as pltpu

F32 = jnp.float32
BF16 = jnp.bfloat16
HIGHEST = lax.Precision.HIGHEST

D = 2048
EPS = 1e-6
RG_W = D // 4
RG_BLOCKS = 8
RG_C = 8.0
DA_W = D // 2
DA_H = 8
DA_DV = 128
DA_DK = 64
ML_W = D // 4
ML_H = 4
ML_DH = 128
N_EXP = 32
TOP_K = 4
FF = D
SWIGLU_LIMIT = 7.0
SWIGLU_ALPHA = 1.702
GRID_W = 64
ROPE_THETA = 10000.0
N_MAIN = 2 * RG_W + 3 * DA_W + 4 * ML_W
N_GATE = 4 * ML_H

LANE = 128
SUBLANE = 8

TM_IN = 1024
TN_IN = 1024
NORM_ROWS = 256
TM_OUT = 512
CHUNK = 256
TQ = 512
KV_STEP = 512
KV_CHUNKS = (1536, 1024, 512)
TM_MOE = 1536
TF_MOE = 256
TN_MOE = 256
TM_DSP = 256
TM_EW = 512
TM_CMB = 128
VMEM_LIMIT = 60 * 1024 * 1024
LOG2E = 1.4426950408889634


def _cparams(sem, **kw):
    return pltpu.CompilerParams(dimension_semantics=sem, vmem_limit_bytes=VMEM_LIMIT, **kw)


def _rms(x, eps=EPS):
    return x * lax.rsqrt(jnp.mean(x * x, axis=-1, keepdims=True) + eps)


def _log_sigmoid(x):
    return jnp.minimum(x, 0.0) - jnp.log1p(jnp.exp(-jnp.abs(x)))


class Geo:
    def __init__(self, n_ctx, s_ctx, n_lat, s_lat):
        self.n_ctx, self.s_ctx, self.n_lat, self.s_lat = n_ctx, s_ctx, n_lat, s_lat
        self.t_ctx = n_ctx * s_ctx
        self.t_lat = n_lat * s_lat
        self.t = self.t_ctx + self.t_lat
        self.n_seq = n_ctx + n_lat
        self.cond_rows = -(-(n_lat + 1) // SUBLANE) * SUBLANE
        self.ctx_cond_row = n_lat


def _ada_kernel(c_ref, w_ref, b_ref, o_ref):
    c = c_ref[...]
    s = (c * jax.nn.sigmoid(c)).astype(BF16)
    o_ref[...] = jnp.dot(s, w_ref[...].astype(BF16), preferred_element_type=F32) + b_ref[...]


def ada_modulation(cond, ada_w, ada_b):
    n_layer, _, n6 = ada_w.shape
    rows = cond.shape[0]
    tn = 1536
    return pl.pallas_call(
        _ada_kernel,
        grid=(n_layer, n6 // tn),
        in_specs=[pl.BlockSpec((rows, D), lambda l, j: (0, 0)),
                  pl.BlockSpec((None, D, tn), lambda l, j: (l, 0, j)),
                  pl.BlockSpec((None, 1, tn), lambda l, j: (l, 0, j))],
        out_specs=pl.BlockSpec((None, rows, tn), lambda l, j: (l, 0, j)),
        out_shape=jax.ShapeDtypeStruct((n_layer, rows, n6), F32),
        compiler_params=_cparams(("arbitrary", "arbitrary")),
        name="ada_modulation",
    )(cond, ada_w, ada_b.reshape(n_layer, 1, n6))


def _cond_row(i, geo, tm):
    n_ctx_blk = geo.t_ctx // tm
    return jnp.where(i < n_ctx_blk, geo.ctx_cond_row, (i - n_ctx_blk) // (geo.s_lat // tm))


def _inproj_kernel(x_ref, mod_ref, g_ref, w_ref, wg_ref, cos_ref, sin_ref,
                   zrg_ref, z_ref, kv_ref, gt_ref, h_scr, *, geo):
    i = pl.program_id(0)
    j = pl.program_id(1)

    @pl.when(j == 0)
    def _():
        r = _cond_row(i, geo, TM_IN)
        gain = g_ref[...] * (1.0 + mod_ref[pl.ds(r, 1), D:2 * D])
        shift = mod_ref[pl.ds(r, 1), 0:D]
        for r0 in range(0, TM_IN, NORM_ROWS):
            rows = slice(r0, r0 + NORM_ROWS)
            h_scr[rows, :] = (_rms(x_ref[rows, :]) * gain + shift).astype(BF16)
        gt_ref[...] = jnp.dot(h_scr[...], wg_ref[...], preferred_element_type=F32)

    q_tile0 = (2 * RG_W) // TN_IN
    k_tile = q_tile0 + DA_W // TN_IN
    is_qk = jnp.logical_and(j >= q_tile0, j < q_tile0 + (2 * DA_W) // TN_IN)
    is_kv = jnp.logical_and(j >= k_tile, j < k_tile + (2 * DA_W) // TN_IN)

    for n0 in range(0, TN_IN, TN_IN // 2):
        cols = slice(n0, n0 + TN_IN // 2)
        z = jnp.dot(h_scr[...], w_ref[:, cols], preferred_element_type=F32)

        @pl.when(j == 0)
        def _():
            zrg_ref[:, cols] = z

        @pl.when(is_qk)
        def _():
            qs = jnp.where(j == q_tile0, DA_DK ** -0.5 * LOG2E, 1.0)
            c = cos_ref[...] * qs
            s = sin_ref[...] * qs
            lane = lax.broadcasted_iota(jnp.int32, (TM_IN, LANE), 1)
            first = (lane % (DA_DK // 2)) < (DA_DK // 4)
            for k in range(TN_IN // 2 // LANE):
                zk = z[:, k * LANE:(k + 1) * LANE]
                partner = jnp.where(first, pltpu.roll(zk, LANE - DA_DK // 4, 1), pltpu.roll(zk, DA_DK // 4, 1))
                z_ref[:, n0 + k * LANE:n0 + (k + 1) * LANE] = (zk * c + partner * s).astype(BF16)

        @pl.when(jnp.logical_and(j > 0, jnp.logical_not(is_qk)))
        def _():
            z_ref[:, cols] = z.astype(BF16)

        @pl.when(jnp.logical_and(is_kv, i < geo.t_ctx // TM_IN))
        def _():
            kv_ref[:, cols] = z


def in_projection(x, mod, norm_g, w_in_bf, w_gate_bf, rope_cos, rope_sin, layer, geo):
    assert TN_IN == 2 * RG_W == DA_W
    n_i = geo.t // TM_IN
    n_j = N_MAIN // TN_IN
    n_ctx_blk = geo.t_ctx // TM_IN
    lat_blk = geo.s_lat // TM_IN

    def rope_idx(i, j):
        return (jnp.where(i < n_ctx_blk, 0, 1 + (i - n_ctx_blk) % lat_blk), 0)

    def kv_idx(i, j):
        k_tile = (2 * RG_W + DA_W) // TN_IN
        return (jnp.minimum(i, n_ctx_blk - 1), jnp.where(i < n_ctx_blk, jnp.clip(j - k_tile, 0, 1), 1))

    return pl.pallas_call(
        functools.partial(_inproj_kernel, geo=geo),
        grid=(n_i, n_j),
        in_specs=[pl.BlockSpec((TM_IN, D), lambda i, j: (i, 0)),
                  pl.BlockSpec((None, geo.cond_rows, 6 * D), lambda i, j: (layer, 0, 0)),
                  pl.BlockSpec((None, 1, D), lambda i, j: (layer, 0, 0)),
                  pl.BlockSpec((None, D, TN_IN), lambda i, j: (layer, 0, j)),
                  pl.BlockSpec((None, D, N_GATE), lambda i, j: (layer, 0, 0)),
                  pl.BlockSpec((TM_IN, LANE), rope_idx),
                  pl.BlockSpec((TM_IN, LANE), rope_idx)],
        out_specs=[pl.BlockSpec((TM_IN, TN_IN), lambda i, j: (i, 0)),
                   pl.BlockSpec((TM_IN, TN_IN), lambda i, j: (i, jnp.maximum(j - 1, 0))),
                   pl.BlockSpec((TM_IN, TN_IN), kv_idx),
                   pl.BlockSpec((TM_IN, N_GATE), lambda i, j: (i, 0))],
        out_shape=[jax.ShapeDtypeStruct((geo.t, TN_IN), F32),
                   jax.ShapeDtypeStruct((geo.t, N_MAIN - TN_IN), BF16),
                   jax.ShapeDtypeStruct((geo.t_ctx, 2 * DA_W), F32),
                   jax.ShapeDtypeStruct((geo.t, N_GATE), F32)],
        scratch_shapes=[pltpu.VMEM((TM_IN, D), BF16)],
        compiler_params=_cparams(("arbitrary", "arbitrary")),
        name="in_projection",
    )(x, mod, norm_g, w_in_bf, w_gate_bf, rope_cos, rope_sin)


def rope_tables(s_lat):
    t = jnp.arange(s_lat)
    row = (t // GRID_W).astype(F32)
    col = (t % GRID_W).astype(F32)
    d = jnp.arange(LANE) % DA_DK
    axis = d // (DA_DK // 2)
    n_freq = DA_DK // 4
    inv = ROPE_THETA ** (-(d % n_freq).astype(F32) / n_freq)
    second = (d % (DA_DK // 2)) >= n_freq
    pos = jnp.where(axis[None, :] == 0, row[:, None], col[:, None])
    ang = pos * inv[None, :]
    cos = jnp.cos(ang)
    sin = jnp.sin(ang) * jnp.where(second, 1.0, -1.0)[None, :]
    cos = jnp.concatenate([jnp.ones((TM_IN, LANE), F32), cos], axis=0)
    sin = jnp.concatenate([jnp.zeros((TM_IN, LANE), F32), sin], axis=0)
    return cos, sin


def _chunk_info(g, geo):
    ncc = geo.t_ctx // CHUNK
    cps_c = geo.s_ctx // CHUNK
    cps_l = geo.s_lat // CHUNK
    is_ctx = g < ncc
    sid = jnp.where(is_ctx, g // cps_c, geo.n_ctx + (g - ncc) // cps_l)
    pos = jnp.where(is_ctx, g % cps_c, (g - ncc) % cps_l)
    last = jnp.where(is_ctx, cps_c - 1, cps_l - 1)
    return sid, pos == 0, pos == last


def _rglru_kernel(*refs, geo, rev, n_chunk):
    if rev:
        (xp_ref, x_ref, xn_ref, cw_ref, cb_ref, wg_ref, bg_ref, lam_ref, h0_ref,
         h_ref, st_ref, pad_scr, a_scr, u_scr, carry_scr) = refs
    else:
        (xp_ref, x_ref, xn_ref, y_ref, hb_ref, cw_ref, cb_ref, wg_ref, bg_ref, lam_ref, h0_ref,
         h_ref, st_ref, pad_scr, a_scr, u_scr, carry_scr) = refs
    p = pl.program_id(0)
    g = n_chunk - 1 - p if rev else p
    _, t_first, t_last = _chunk_info(g, geo)
    starts = t_last if rev else t_first

    pad_scr[0:SUBLANE, :] = jnp.where(t_first, 0.0, xp_ref[...])
    pad_scr[SUBLANE:SUBLANE + CHUNK, :] = x_ref[...]
    pad_scr[SUBLANE + CHUNK:2 * SUBLANE + CHUNK, :] = jnp.where(t_last, 0.0, xn_ref[...])
    xc = cb_ref[...]
    for tap in range(4):
        xc = xc + cw_ref[tap:tap + 1, :] * pad_scr[pl.ds(SUBLANE - 2 + tap, CHUNK), :]

    gates = jnp.dot(xc.astype(BF16), wg_ref[...], preferred_element_type=F32) + bg_ref[...]
    r = jax.nn.sigmoid(gates[:, :RG_W])
    i = jax.nn.sigmoid(gates[:, RG_W:])
    log_a = (RG_C * _log_sigmoid(lam_ref[...])) * r
    a = jnp.exp(log_a)
    u = jnp.sqrt(-jnp.tanh(log_a) * (a * a + 1.0)) * (i * xc)

    row = lax.broadcasted_iota(jnp.int32, (CHUNK, RG_W), 0) % SUBLANE
    s = 1
    while s < SUBLANE:
        shift = CHUNK - s if rev else s
        valid = (row < SUBLANE - s) if rev else (row >= s)
        a_sh = pltpu.roll(a, shift, 0)
        u_sh = pltpu.roll(u, shift, 0)
        u = jnp.where(valid, u + a * u_sh, u)
        a = jnp.where(valid, a * a_sh, a)
        s *= 2
    a_scr[...] = a
    u_scr[...] = u

    @pl.when(starts)
    def _():
        carry_scr[...] = h0_ref[...]

    n_grp = CHUNK // SUBLANE

    def body(k, carry):
        grp = n_grp - 1 - k if rev else k
        rows = pl.ds(pl.multiple_of(grp * SUBLANE, SUBLANE), SUBLANE)
        h8 = u_scr[rows, :] + a_scr[rows, :] * carry
        u_scr[rows, :] = h8
        return h8[0:1, :] if rev else h8[SUBLANE - 1:SUBLANE, :]

    carry = lax.fori_loop(0, n_grp, body, carry_scr[...])
    carry_scr[...] = carry
    st_ref[...] = carry
    if rev:
        h_ref[...] = u_scr[...]
    else:
        h_ref[...] = (jax.nn.gelu(y_ref[...]) * (u_scr[...] + hb_ref[...])).astype(BF16)


def rglru_direction(z, hb, conv_w, conv_b, gate_w_bf, gate_b, lam, h0, layer, geo, rev):
    n_chunk = geo.t // CHUNK
    d = 1 if rev else 0
    cpb = CHUNK // SUBLANE
    n_blk8 = geo.t // SUBLANE

    def gi(p):
        return n_chunk - 1 - p if rev else p

    def sid(p):
        return _chunk_info(gi(p), geo)[0]

    x_specs = [pl.BlockSpec((SUBLANE, RG_W), lambda p: (jnp.maximum(gi(p) * cpb - 1, 0), 0)),
               pl.BlockSpec((CHUNK, RG_W), lambda p: (gi(p), 0)),
               pl.BlockSpec((SUBLANE, RG_W), lambda p: (jnp.minimum((gi(p) + 1) * cpb, n_blk8 - 1), 0))]
    w_specs = [pl.BlockSpec((None, 4, RG_W), lambda p: (layer, 0, 0)),
               pl.BlockSpec((None, 1, RG_W), lambda p: (layer, 0, 0)),
               pl.BlockSpec((None, None, RG_W, 2 * RG_W), lambda p: (layer, d, 0, 0)),
               pl.BlockSpec((None, None, 1, 2 * RG_W), lambda p: (layer, d, 0, 0)),
               pl.BlockSpec((None, None, 1, RG_W), lambda p: (layer, d, 0, 0)),
               pl.BlockSpec((None, None, 1, RG_W), lambda p: (sid(p), d, 0, 0))]
    if rev:
        in_specs = x_specs + w_specs
        args = (z, z, z, conv_w, conv_b, gate_w_bf, gate_b, lam, h0)
        h_dtype = F32
    else:
        in_specs = x_specs + [pl.BlockSpec((CHUNK, RG_W), lambda p: (gi(p), 1)),
                              pl.BlockSpec((CHUNK, RG_W), lambda p: (gi(p), 0))] + w_specs
        args = (z, z, z, z, hb, conv_w, conv_b, gate_w_bf, gate_b, lam, h0)
        h_dtype = BF16
    return pl.pallas_call(
        functools.partial(_rglru_kernel, geo=geo, rev=rev, n_chunk=n_chunk),
        grid=(n_chunk,),
        in_specs=in_specs,
        out_specs=[pl.BlockSpec((CHUNK, RG_W), lambda p: (gi(p), 0)),
                   pl.BlockSpec((None, 1, RG_W), lambda p: (sid(p), 0, 0))],
        out_shape=[jax.ShapeDtypeStruct((geo.t, RG_W), h_dtype),
                   jax.ShapeDtypeStruct((geo.n_seq, 1, RG_W), F32)],
        scratch_shapes=[pltpu.VMEM((CHUNK + 2 * SUBLANE, RG_W), F32),
                        pltpu.VMEM((CHUNK, RG_W), F32),
                        pltpu.VMEM((CHUNK, RG_W), F32),
                        pltpu.VMEM((1, RG_W), F32)],
        compiler_params=_cparams(("arbitrary",)),
        name="rglru_bwd" if rev else "rglru_fwd",
    )(*args)


def _attn_kernel(*refs, past, n_new, kv_chunk, lam_init):
    if past:
        q_ref, kc_ref, vc_ref, k_ref, v_ref, lam_ref, g_ref, o_ref, kt_scr, v_scr = refs
    else:
        q_ref, k_ref, v_ref, lam_ref, g_ref, o_ref, kt_scr, v_scr = refs
    n_kv = past + n_new
    step = min(KV_STEP, n_new)

    @pl.when(pl.program_id(2) == 0)
    def _():
        for r0 in range(0, past, step):
            rows = slice(r0, r0 + step)
            kt_scr[:, rows] = kc_ref[rows, :].T.astype(BF16)
            v_scr[rows, 0:DA_DV] = vc_ref[rows, :].astype(BF16)
        for r0 in range(0, n_new, step):
            src = slice(r0, r0 + step)
            dst = slice(past + r0, past + r0 + step)
            kt_scr[:, dst] = k_ref[src, :].astype(F32).T.astype(BF16)
            v_scr[dst, 0:DA_DV] = v_ref[src, :]
        v_scr[:, DA_DV:2 * DA_DV] = jnp.ones((n_kv, DA_DV), BF16)

    q = q_ref[...].astype(F32)
    lane = lax.broadcasted_iota(jnp.int32, q.shape, 1)
    heads = []
    for c in range(2):
        in_comp = (lane < DA_DK) if c == 0 else (lane >= DA_DK)
        qc = jnp.where(in_comp, q, 0.0).astype(BF16)
        m = None
        acc = None
        for c0 in range(0, n_kv, kv_chunk):
            cols = slice(c0, c0 + kv_chunk)
            s = jnp.dot(qc, kt_scr[:, cols], preferred_element_type=F32)
            mx = jnp.max(s, axis=-1, keepdims=True)
            m_new = mx if m is None else jnp.maximum(m, mx)
            pv = jnp.dot(jnp.exp2(s - m_new).astype(BF16), v_scr[cols, :], preferred_element_type=F32)
            acc = pv if m is None else jnp.exp2(m - m_new) * acc + pv
            m = m_new
        heads.append(acc[:, 0:DA_DV] / acc[:, DA_DV:2 * DA_DV])

    lq = lam_ref[...]
    lam = (jnp.exp(jnp.sum(lq[0:1] * lq[1:2], axis=-1, keepdims=True))
           - jnp.exp(jnp.sum(lq[2:3] * lq[3:4], axis=-1, keepdims=True)) + lam_init)
    o = heads[0] - lam * heads[1]
    o_ref[...] = (_rms(o) * g_ref[...] * (1.0 - lam_init)).astype(BF16)


def _kv_chunk(n_kv):
    for cand in KV_CHUNKS:
        if n_kv % cand == 0:
            return cand
    return n_kv


def _attn_scratch(n_kv):
    return [pltpu.VMEM((2 * DA_DK, n_kv), BF16), pltpu.VMEM((n_kv, 2 * DA_DV), BF16)]


_Q_COL = 0
_K_COL = _Q_COL + DA_W // LANE
_V_COL = _K_COL + DA_W // LANE


def attention_context(z, da_lambda, da_norm_g, layer, lam_init, geo):
    s = geo.s_ctx
    return pl.pallas_call(
        functools.partial(_attn_kernel, past=0, n_new=s, kv_chunk=_kv_chunk(s), lam_init=lam_init),
        grid=(geo.n_ctx, DA_H, 1),
        in_specs=[pl.BlockSpec((s, LANE), lambda b, h, qi: (b, _Q_COL + h)),
                  pl.BlockSpec((s, LANE), lambda b, h, qi: (b, _K_COL + h)),
                  pl.BlockSpec((s, LANE), lambda b, h, qi: (b, _V_COL + h)),
                  pl.BlockSpec((None, 4, DA_DK), lambda b, h, qi: (layer, 0, 0)),
                  pl.BlockSpec((None, 1, DA_DV), lambda b, h, qi: (layer, 0, 0))],
        out_specs=pl.BlockSpec((s, LANE), lambda b, h, qi: (b, h)),
        out_shape=jax.ShapeDtypeStruct((geo.t_ctx, DA_W), BF16),
        scratch_shapes=_attn_scratch(s),
        compiler_params=_cparams(("arbitrary",) * 3),
        name="attention_context",
    )(z, z, z, da_lambda, da_norm_g)


def attention_latent(z, cache_k, cache_v, da_lambda, da_norm_g, layer, lam_init, geo):
    past = cache_k.shape[2]
    s = geo.s_lat
    n_kv = past + s
    nq = s // TQ
    q0 = geo.t_ctx // TQ
    k0 = geo.t_ctx // s
    return pl.pallas_call(
        functools.partial(_attn_kernel, past=past, n_new=s, kv_chunk=_kv_chunk(n_kv), lam_init=lam_init),
        grid=(geo.n_lat, DA_H, nq),
        in_specs=[pl.BlockSpec((TQ, LANE), lambda b, h, qi: (q0 + b * nq + qi, _Q_COL + h)),
                  pl.BlockSpec((None, None, past, LANE), lambda b, h, qi: (b, layer, 0, h)),
                  pl.BlockSpec((None, None, past, LANE), lambda b, h, qi: (b, layer, 0, h)),
                  pl.BlockSpec((s, LANE), lambda b, h, qi: (k0 + b, _K_COL + h)),
                  pl.BlockSpec((s, LANE), lambda b, h, qi: (k0 + b, _V_COL + h)),
                  pl.BlockSpec((None, 4, DA_DK), lambda b, h, qi: (layer, 0, 0)),
                  pl.BlockSpec((None, 1, DA_DV), lambda b, h, qi: (layer, 0, 0))],
        out_specs=pl.BlockSpec((TQ, LANE), lambda b, h, qi: (b * nq + qi, h)),
        out_shape=jax.ShapeDtypeStruct((geo.t_lat, DA_W), BF16),
        scratch_shapes=_attn_scratch(n_kv),
        compiler_params=_cparams(("arbitrary",) * 3),
        name="attention_latent",
    )(z, cache_k, cache_v, z, z, da_lambda, da_norm_g)


def _mlstm_kernel(*refs, geo, rev, n_chunk):
    if rev:
        (q_ref, k_ref, v_ref, gt_ref, gb_ref, c0_ref, n0_ref, m0_ref,
         h_ref, co_ref, no_ref, mo_ref, c_scr, n_scr, m_scr) = refs
    else:
        (q_ref, k_ref, v_ref, o_ref, hb_ref, gt_ref, gb_ref, ng_ref, c0_ref, n0_ref, m0_ref,
         h_ref, co_ref, no_ref, mo_ref, c_scr, n_scr, m_scr) = refs
    p = pl.program_id(0)
    g = n_chunk - 1 - p if rev else p
    _, t_first, t_last = _chunk_info(g, geo)
    starts = t_last if rev else t_first
    d = 1 if rev else 0

    @pl.when(starts)
    def _():
        c_scr[...] = c0_ref[...]
        n_scr[...] = n0_ref[...]
        m_scr[...] = m0_ref[...]

    gt = gt_ref[...] + gb_ref[...]
    eye = jnp.where(lax.broadcasted_iota(jnp.int32, (N_GATE, N_GATE), 0)
                    == lax.broadcasted_iota(jnp.int32, (N_GATE, N_GATE), 1), 1.0, 0.0)
    gtt = lax.dot_general(eye, gt, (((1,), (1,)), ((), ())), precision=HIGHEST, preferred_element_type=F32)
    ti = lax.broadcasted_iota(jnp.int32, (CHUNK, CHUNK), 0)
    si = lax.broadcasted_iota(jnp.int32, (CHUNK, CHUNK), 1)
    keep = (si >= ti) if rev else (si <= ti)
    tri = jnp.where(keep, 1.0, 0.0)
    b_col_all = jnp.dot(tri, _log_sigmoid(gt), precision=HIGHEST, preferred_element_type=F32)
    tri_t = jnp.where((ti >= si) if rev else (ti <= si), 1.0, 0.0)
    b_row_all = jnp.dot(_log_sigmoid(gtt), tri_t, precision=HIGHEST, preferred_element_type=F32)
    edge = 0 if rev else CHUNK - 1

    for h in range(ML_H):
        ci = d * 2 * ML_H + h
        cf = ci + ML_H
        hs = slice(h * ML_DH, (h + 1) * ML_DH)
        qb = q_ref[:, hs]
        vb = v_ref[:, hs]
        qh = qb.astype(F32)
        kh = k_ref[:, hs].astype(F32) * (ML_DH ** -0.5)
        vh = vb.astype(F32)
        kb = kh.astype(BF16)
        li_col = gt[:, ci:ci + 1]
        li_row = gtt[ci:ci + 1, :]
        b_col = b_col_all[:, cf:cf + 1]
        b_row = b_row_all[cf:cf + 1, :]
        m_prev = m_scr[h][:, 0:1]
        c_prev = c_scr[h]
        n_prev = n_scr[h]

        dm = jnp.where(keep, b_col - b_row + li_row, -jnp.inf)
        inter = b_col + m_prev
        m_t = jnp.maximum(inter, jnp.max(dm, axis=-1, keepdims=True))
        s = lax.dot_general(qb, kb, (((1,), (1,)), ((), ())), preferred_element_type=F32) * jnp.exp(dm - m_t)
        w_inter = jnp.exp(inter - m_t)
        cq = lax.dot_general(qb, c_prev.astype(BF16), (((1,), (1,)), ((), ())), preferred_element_type=F32)
        num = jnp.dot(s.astype(BF16), vb, preferred_element_type=F32) + w_inter * cq
        den = jnp.sum(s, axis=-1, keepdims=True) + w_inter * jnp.sum(qh * n_prev, axis=-1, keepdims=True)
        hh = num / jnp.maximum(jnp.abs(den), jnp.exp(-m_t))

        b_tot = b_col[edge:edge + 1, :]
        gg = b_tot - b_col + li_col
        m_new = jnp.maximum(b_tot + m_prev, jnp.max(gg, axis=0, keepdims=True))
        wgt = jnp.exp(gg - m_new)
        decay = jnp.exp(b_tot + m_prev - m_new)
        wv_t = (wgt * vh).T.astype(BF16)
        c_new = decay * c_prev + jnp.dot(wv_t, kb, preferred_element_type=F32)
        n_new = decay * n_prev + jnp.sum(wgt * kh, axis=0, keepdims=True)
        c_scr[h] = c_new
        n_scr[h] = n_new
        m_scr[h] = jnp.broadcast_to(m_new, (1, ML_DH))
        co_ref[h] = c_new
        no_ref[h] = n_new
        mo_ref[h] = jnp.broadcast_to(m_new, (1, ML_DH))

        if rev:
            h_ref[:, hs] = hh
        else:
            gated = jax.nn.sigmoid(o_ref[:, hs].astype(F32)) * (hh + hb_ref[:, hs])
            h_ref[:, hs] = (_rms(gated) * ng_ref[:, hs]).astype(BF16)


_MLQ_COL = (3 * DA_W) // ML_W


def mlstm_direction(z, hb, gates, gate_b, norm_g, c0, n0, m0, layer, geo, rev):
    n_chunk = geo.t // CHUNK
    d = 1 if rev else 0

    def gi(p):
        return n_chunk - 1 - p if rev else p

    def sid(p):
        return _chunk_info(gi(p), geo)[0]

    def zspec(col):
        return pl.BlockSpec((CHUNK, ML_W), lambda p: (gi(p), col))

    qkv = [zspec(_MLQ_COL), zspec(_MLQ_COL + 1), zspec(_MLQ_COL + 2)]
    gate_specs = [pl.BlockSpec((CHUNK, N_GATE), lambda p: (gi(p), 0)),
                  pl.BlockSpec((None, 1, N_GATE), lambda p: (layer, 0, 0))]
    st_specs = [pl.BlockSpec((None, None, ML_H, ML_DH, ML_DH), lambda p: (sid(p), d, 0, 0, 0)),
                pl.BlockSpec((None, None, ML_H, 1, ML_DH), lambda p: (sid(p), d, 0, 0, 0)),
                pl.BlockSpec((None, None, ML_H, 1, ML_DH), lambda p: (sid(p), d, 0, 0, 0))]
    if rev:
        in_specs = qkv + gate_specs + st_specs
        args = (z, z, z, gates, gate_b, c0, n0, m0)
        h_dtype = F32
    else:
        in_specs = (qkv + [zspec(_MLQ_COL + 3), pl.BlockSpec((CHUNK, ML_W), lambda p: (gi(p), 0))] + gate_specs
                    + [pl.BlockSpec((None, 1, ML_W), lambda p: (layer, 0, 0))] + st_specs)
        args = (z, z, z, z, hb, gates, gate_b, norm_g, c0, n0, m0)
        h_dtype = BF16
    return pl.pallas_call(
        functools.partial(_mlstm_kernel, geo=geo, rev=rev, n_chunk=n_chunk),
        grid=(n_chunk,),
        in_specs=in_specs,
        out_specs=[pl.BlockSpec((CHUNK, ML_W), lambda p: (gi(p), 0)),
                   pl.BlockSpec((None, ML_H, ML_DH, ML_DH), lambda p: (sid(p), 0, 0, 0)),
                   pl.BlockSpec((None, ML_H, 1, ML_DH), lambda p: (sid(p), 0, 0, 0)),
                   pl.BlockSpec((None, ML_H, 1, ML_DH), lambda p: (sid(p), 0, 0, 0))],
        out_shape=[jax.ShapeDtypeStruct((geo.t, ML_W), h_dtype),
                   jax.ShapeDtypeStruct((geo.n_seq, ML_H, ML_DH, ML_DH), F32),
                   jax.ShapeDtypeStruct((geo.n_seq, ML_H, 1, ML_DH), F32),
                   jax.ShapeDtypeStruct((geo.n_seq, ML_H, 1, ML_DH), F32)],
        scratch_shapes=[pltpu.VMEM((ML_H, ML_DH, ML_DH), F32),
                        pltpu.VMEM((ML_H, 1, ML_DH), F32),
                        pltpu.VMEM((ML_H, 1, ML_DH), F32)],
        compiler_params=_cparams(("arbitrary",)),
        name="mlstm_bwd" if rev else "mlstm_fwd",
    )(*args)


def _outproj_kernel(x_ref, rg_ref, dac_ref, dal_ref, ml_ref, w_ref, mod_ref, g_ref, wr_ref, br_ref,
                    x1_ref, h2_ref, te_ref, tw_ref, *, geo):
    i = pl.program_id(0)
    r = _cond_row(i, geo, TM_OUT)
    gate1 = mod_ref[pl.ds(r, 1), 2 * D:3 * D]
    shift2 = mod_ref[pl.ds(r, 1), 3 * D:4 * D]
    scale2 = mod_ref[pl.ds(r, 1), 4 * D:5 * D]
    da = jnp.where(i < geo.t_ctx // TM_OUT, dac_ref[...], dal_ref[...])
    y = jnp.dot(rg_ref[...], w_ref[0:RG_W, :], preferred_element_type=F32)
    y = y + jnp.dot(da, w_ref[RG_W:RG_W + DA_W, :], preferred_element_type=F32)
    y = y + jnp.dot(ml_ref[...], w_ref[RG_W + DA_W:D, :], preferred_element_type=F32)
    x1 = x_ref[...] + gate1 * y
    x1_ref[...] = x1
    h2 = _rms(x1) * g_ref[...] * (1.0 + scale2) + shift2
    lo = lax.bitcast_convert_type(h2[:, 0:D // 2].astype(BF16).astype(F32), jnp.uint32) >> 16
    hi = lax.bitcast_convert_type(h2[:, D // 2:D].astype(BF16).astype(F32), jnp.uint32)
    h2_ref[...] = hi | lo

    logits = jnp.dot(h2.astype(BF16), wr_ref[...], preferred_element_type=F32) + br_ref[...]
    lane = lax.broadcasted_iota(jnp.int32, logits.shape, 1)
    lane_f = lane.astype(F32)
    sel_e = jnp.zeros(logits.shape, jnp.int32)
    sel_v = jnp.zeros(logits.shape, F32)
    top0 = None
    total = None
    for k in range(TOP_K):
        mk = jnp.max(logits, axis=-1, keepdims=True)
        idx = jnp.min(jnp.where(logits == mk, lane_f, float(LANE)), axis=-1, keepdims=True)
        if k == 0:
            top0 = mk
        ek = jnp.exp(mk - top0)
        total = ek if total is None else total + ek
        sel_e = jnp.where(lane == k, idx.astype(jnp.int32), sel_e)
        sel_v = jnp.where(lane == k, ek, sel_v)
        logits = jnp.where(lane_f == idx, -jnp.inf, logits)
    te_ref[...] = sel_e
    tw_ref[...] = sel_v / total


def out_projection(x, out_rg, da_ctx, da_lat, out_ml, w_out_bf, mod, norm_g, router_w, router_b, layer, geo):
    n_i = geo.t // TM_OUT
    n_c = geo.t_ctx // TM_OUT
    return pl.pallas_call(
        functools.partial(_outproj_kernel, geo=geo),
        grid=(n_i,),
        in_specs=[pl.BlockSpec((TM_OUT, D), lambda i: (i, 0)),
                  pl.BlockSpec((TM_OUT, RG_W), lambda i: (i, 0)),
                  pl.BlockSpec((TM_OUT, DA_W), lambda i: (jnp.minimum(i, n_c - 1), 0)),
                  pl.BlockSpec((TM_OUT, DA_W), lambda i: (jnp.maximum(i - n_c, 0), 0)),
                  pl.BlockSpec((TM_OUT, ML_W), lambda i: (i, 0)),
                  pl.BlockSpec((None, D, D), lambda i: (layer, 0, 0), pipeline_mode=pl.Buffered(1)),
                  pl.BlockSpec((None, geo.cond_rows, 6 * D), lambda i: (layer, 0, 0), pipeline_mode=pl.Buffered(1)),
                  pl.BlockSpec((None, 1, D), lambda i: (layer, 0, 0)),
                  pl.BlockSpec((None, D, LANE), lambda i: (layer, 0, 0), pipeline_mode=pl.Buffered(1)),
                  pl.BlockSpec((None, 1, LANE), lambda i: (layer, 0, 0))],
        out_specs=[pl.BlockSpec((TM_OUT, D), lambda i: (i, 0)),
                   pl.BlockSpec((TM_OUT, D // 2), lambda i: (i, 0)),
                   pl.BlockSpec((TM_OUT, LANE), lambda i: (i, 0)),
                   pl.BlockSpec((TM_OUT, LANE), lambda i: (i, 0))],
        out_shape=[jax.ShapeDtypeStruct((geo.t, D), F32),
                   jax.ShapeDtypeStruct((geo.t, D // 2), jnp.uint32),
                   jax.ShapeDtypeStruct((geo.t, LANE), jnp.int32),
                   jax.ShapeDtypeStruct((geo.t, LANE), F32)],
        compiler_params=_cparams(("arbitrary",)),
        name="out_projection",
    )(x, out_rg, da_ctx, da_lat, out_ml, w_out_bf, mod, norm_g, router_w, router_b)


def _dispatch_kernel(dest_ref, h_ref, xs_in, xs_out, sem):
    del xs_in

    def body(r, carry):
        for k in range(TOP_K):
            d = dest_ref[r * TOP_K + k]
            pltpu.make_async_copy(h_ref.at[pl.ds(r, 1)], xs_out.at[pl.ds(d, 1)], sem).start()
        return carry

    lax.fori_loop(0, TM_DSP, body, 0)
    for k in range(TOP_K):
        pltpu.make_async_copy(h_ref, xs_out.at[pl.ds(0, TM_DSP)], sem).wait()


def moe_dispatch(dest, h2_packed, n_rows):
    t = h2_packed.shape[0]
    return pl.pallas_call(
        _dispatch_kernel,
        grid=(t // TM_DSP,),
        in_specs=[pl.BlockSpec((TM_DSP * TOP_K,), lambda i: (i,), memory_space=pltpu.SMEM),
                  pl.BlockSpec((TM_DSP, D // 2), lambda i: (i, 0)),
                  pl.BlockSpec(memory_space=pl.ANY)],
        out_specs=pl.BlockSpec(memory_space=pl.ANY),
        out_shape=jax.ShapeDtypeStruct((n_rows, D // 2), jnp.uint32),
        scratch_shapes=[pltpu.SemaphoreType.DMA(())],
        input_output_aliases={2: 0},
        compiler_params=_cparams(("arbitrary",), disable_bounds_checks=True),
        name="moe_dispatch",
    )(dest, h2_packed, jnp.zeros((n_rows, D // 2), jnp.uint32))


def _moe_kernel(be_ref, nu_ref, x_ref, wg_ref, wu_ref, bg_ref, bu_ref, wd_ref, bd_ref, o_ref, x_scr, act_scr, *, n_a):
    m = pl.program_id(0)
    s = pl.program_id(1)
    used = m < nu_ref[0]
    half = D // 2

    @pl.when(jnp.logical_and(used, s == 0))
    def _():
        u = x_ref[...]
        x_scr[:, 0:half] = lax.bitcast_convert_type(u << 16, F32).astype(BF16)
        x_scr[:, half:D] = lax.bitcast_convert_type((u >> 16) << 16, F32).astype(BF16)

    @pl.when(jnp.logical_and(used, s < n_a))
    def _():
        x = x_scr[...]
        gate = jnp.dot(x, wg_ref[...].astype(BF16), preferred_element_type=F32) + bg_ref[...]
        up = jnp.dot(x, wu_ref[...].astype(BF16), preferred_element_type=F32) + bu_ref[...]
        gate = jnp.minimum(gate, SWIGLU_LIMIT)
        up = jnp.clip(up, -SWIGLU_LIMIT, SWIGLU_LIMIT)
        act = (gate * jax.nn.sigmoid(SWIGLU_ALPHA * gate) * (up + 1.0)).astype(BF16)
        for j in range(n_a):
            @pl.when(s == j)
            def _():
                act_scr[:, j * TF_MOE:(j + 1) * TF_MOE] = act

    @pl.when(jnp.logical_and(used, s >= n_a))
    def _():
        o_ref[...] = jnp.dot(act_scr[...], wd_ref[...].astype(BF16), preferred_element_type=F32) + bd_ref[...]

    @pl.when(jnp.logical_and(jnp.logical_not(used), s >= n_a))
    def _():
        o_ref[...] = jnp.zeros(o_ref.shape, F32)


def moe_experts(x_sorted, blk_e, n_used, w_gu, b_gu, w_dn, b_dn, layer):
    n_rows = x_sorted.shape[0]
    n_blk = n_rows // TM_MOE
    n_a = FF // TF_MOE
    n_b = D // TN_MOE

    def ja(m, s, nu):
        return jnp.where(m < nu[0], jnp.minimum(s, n_a - 1), n_a - 1)

    def jb(m, s, nu):
        return jnp.where(m < nu[0], jnp.maximum(s - n_a, 0), n_b - 1)

    grid_spec = pltpu.PrefetchScalarGridSpec(
        num_scalar_prefetch=2,
        grid=(n_blk, n_a + n_b),
        in_specs=[pl.BlockSpec((TM_MOE, D // 2), lambda m, s, be, nu: (m, 0)),
                  pl.BlockSpec((None, None, D, TF_MOE), lambda m, s, be, nu: (layer, be[m], 0, ja(m, s, nu))),
                  pl.BlockSpec((None, None, D, TF_MOE), lambda m, s, be, nu: (layer, be[m], 0, n_a + ja(m, s, nu))),
                  pl.BlockSpec((None, None, 1, TF_MOE), lambda m, s, be, nu: (layer, be[m], 0, ja(m, s, nu))),
                  pl.BlockSpec((None, None, 1, TF_MOE), lambda m, s, be, nu: (layer, be[m], 0, n_a + ja(m, s, nu))),
                  pl.BlockSpec((None, None, FF, TN_MOE), lambda m, s, be, nu: (layer, be[m], 0, jb(m, s, nu))),
                  pl.BlockSpec((None, None, 1, TN_MOE), lambda m, s, be, nu: (layer, be[m], 0, jb(m, s, nu)))],
        out_specs=pl.BlockSpec((TM_MOE, TN_MOE), lambda m, s, be, nu: (m, jnp.maximum(s - n_a, 0))),
        scratch_shapes=[pltpu.VMEM((TM_MOE, D), BF16), pltpu.VMEM((TM_MOE, FF), BF16)],
    )
    n_layer, n_exp = w_gu.shape[:2]
    return pl.pallas_call(
        functools.partial(_moe_kernel, n_a=n_a),
        grid_spec=grid_spec,
        out_shape=jax.ShapeDtypeStruct((n_rows, D), F32),
        compiler_params=_cparams(("arbitrary", "arbitrary")),
        name="moe_experts",
    )(blk_e, n_used, x_sorted, w_gu, w_gu, b_gu.reshape(n_layer, n_exp, 1, 2 * FF),
      b_gu.reshape(n_layer, n_exp, 1, 2 * FF), w_dn, b_dn.reshape(n_layer, n_exp, 1, D))


def moe_routing(top_e):
    flat_e = top_e.reshape(-1)
    n_assign = flat_e.shape[0]
    onehot = (flat_e[:, None] == jnp.arange(N_EXP, dtype=jnp.int32)[None, :]).astype(jnp.int32)
    csum = jnp.cumsum(onehot, axis=0)
    counts = csum[-1]
    pos = jnp.take_along_axis(csum, flat_e[:, None], axis=1)[:, 0] - 1
    padded = (counts + TM_MOE - 1) // TM_MOE * TM_MOE
    pad_end = jnp.cumsum(padded)
    pad_start = pad_end - padded
    dest = pad_start[flat_e] + pos
    n_blk = moe_num_blocks(n_assign)
    blk_e = jnp.minimum(jnp.searchsorted(pad_end, jnp.arange(n_blk, dtype=jnp.int32) * TM_MOE, side='right'),
                        N_EXP - 1).astype(jnp.int32)
    n_used = (pad_end[-1] // TM_MOE).astype(jnp.int32).reshape(1)
    return dest.astype(jnp.int32), blk_e, n_used


def moe_num_blocks(n_assign):
    return -(-n_assign // TM_MOE) + N_EXP


def _combine_kernel(dfirst_ref, dnext_ref, x_ref, tw_ref, mod_ref, yb_ref, o_ref, buf, sem, *, geo, n_blk):
    i = pl.program_id(0)

    def gather(d_ref, slot):
        def body(r, carry):
            for k in range(TOP_K):
                d = d_ref[r * TOP_K + k]
                pltpu.make_async_copy(yb_ref.at[pl.ds(d, 1)], buf.at[slot, k, pl.ds(r, 1)], sem.at[slot]).start()
            return carry

        lax.fori_loop(0, TM_CMB, body, 0)

    @pl.when(i == 0)
    def _():
        gather(dfirst_ref, 0)

    @pl.when(i + 1 < n_blk)
    def _():
        gather(dnext_ref, (i + 1) % 2)

    slot = i % 2
    for k in range(TOP_K):
        pltpu.make_async_copy(yb_ref.at[pl.ds(0, TM_CMB)], buf.at[slot, k], sem.at[slot]).wait()

    r = _cond_row(i, geo, TM_CMB)
    gate2 = mod_ref[pl.ds(r, 1), 5 * D:6 * D]
    tw = tw_ref[...]
    y = tw[:, 0:1] * buf[slot, 0]
    for k in range(1, TOP_K):
        y = y + tw[:, k:k + 1] * buf[slot, k]
    o_ref[...] = x_ref[...] + gate2 * y


def moe_combine(dest, x1, top_w, yb, mod, layer, geo):
    n_blk = geo.t // TM_CMB
    dblk = TM_CMB * TOP_K
    return pl.pallas_call(
        functools.partial(_combine_kernel, geo=geo, n_blk=n_blk),
        grid=(n_blk,),
        in_specs=[pl.BlockSpec((dblk,), lambda i: (0,), memory_space=pltpu.SMEM),
                  pl.BlockSpec((dblk,), lambda i: (jnp.minimum(i + 1, n_blk - 1),), memory_space=pltpu.SMEM),
                  pl.BlockSpec((TM_CMB, D), lambda i: (i, 0)),
                  pl.BlockSpec((TM_CMB, LANE), lambda i: (i, 0)),
                  pl.BlockSpec((None, geo.cond_rows, 6 * D), lambda i: (layer, 0, 0)),
                  pl.BlockSpec(memory_space=pl.ANY)],
        out_specs=pl.BlockSpec((TM_CMB, D), lambda i: (i, 0)),
        out_shape=jax.ShapeDtypeStruct((geo.t, D), F32),
        scratch_shapes=[pltpu.VMEM((2, TOP_K, TM_CMB, D), F32), pltpu.SemaphoreType.DMA((2,))],
        compiler_params=_cparams(("arbitrary",), disable_bounds_checks=True),
        name="moe_combine",
    )(dest, dest, x1, top_w, mod, yb)


def _final_norm_kernel(x_ref, g_ref, o_ref):
    o_ref[...] = _rms(x_ref[...]) * g_ref[...]


def final_norm(x, g):
    t = x.shape[0]
    return pl.pallas_call(
        _final_norm_kernel,
        grid=(t // TM_EW,),
        in_specs=[pl.BlockSpec((TM_EW, D), lambda i: (i, 0)), pl.BlockSpec((1, D), lambda i: (0, 0))],
        out_specs=pl.BlockSpec((TM_EW, D), lambda i: (i, 0)),
        out_shape=jax.ShapeDtypeStruct((t, D), F32),
        compiler_params=_cparams(("arbitrary",)),
        name="final_norm",
    )(x, g.reshape(1, D))


def _block_diag(w):
    eye = jnp.eye(RG_BLOCKS, dtype=w.dtype)
    dense = w[..., :, :, None, :] * eye[:, None, :, None]
    return dense.reshape(*w.shape[:-3], RG_W, RG_W)


def kernel(x_prompt, x_sample, c, cache_k, cache_v, state_rglru, state_mlstm_C, state_mlstm_n, state_mlstm_m, c_ctx, ada_w, ada_b, norm1_g, norm2_g, w_in, rg_conv_w, rg_conv_b, rg_gate_w, rg_gate_b, rg_lambda, da_lambda, da_norm_g, ml_gate_b, ml_norm_g, w_out, router_w, router_b, moe_w_gu, moe_b_gu, moe_w_down, moe_b_down, final_g):
    n_ctx, s_ctx, _ = x_prompt.shape
    n_lat, s_lat, _ = x_sample.shape
    n_layer = ada_w.shape[0]
    past = cache_k.shape[2]
    geo = Geo(n_ctx, s_ctx, n_lat, s_lat)
    assert geo.t_ctx % TM_IN == 0 and s_lat % TM_IN == 0 and s_ctx % CHUNK == 0 and s_lat % CHUNK == 0
    assert s_lat % TQ == 0 and geo.t_ctx % s_lat == 0 and past % KV_STEP == 0 and s_ctx % SUBLANE == 0

    x = jnp.concatenate([x_prompt.reshape(geo.t_ctx, D), x_sample.reshape(geo.t_lat, D)], axis=0)
    cond = jnp.zeros((geo.cond_rows, D), F32).at[:n_lat].set(c).at[n_lat].set(c_ctx)
    mod = ada_modulation(cond, ada_w, ada_b)

    w_in_bf = w_in.astype(BF16)
    w_gate_bf = w_in_bf[:, :, N_MAIN:]
    w_out_bf = w_out.astype(BF16)
    rg_gate_dense = jnp.concatenate([_block_diag(rg_gate_w[:, :, 0]), _block_diag(rg_gate_w[:, :, 1])],
                                    axis=-1).astype(BF16)
    rg_gate_bias = rg_gate_b.reshape(n_layer, 2, 1, 2 * RG_W)
    rg_lam = rg_lambda.reshape(n_layer, 2, 1, RG_W)
    rg_cb = rg_conv_b.reshape(n_layer, 1, RG_W)
    ml_gb = ml_gate_b.reshape(n_layer, 1, N_GATE)
    ml_ng = ml_norm_g.reshape(n_layer, 1, ML_W)
    da_ng = da_norm_g.reshape(n_layer, 1, DA_DV)
    n1g = norm1_g.reshape(n_layer, 1, D)
    n2g = norm2_g.reshape(n_layer, 1, D)
    router_w_pad = jnp.pad(router_w, ((0, 0), (0, 0), (0, LANE - N_EXP))).astype(BF16)
    router_b_pad = jnp.pad(router_b, ((0, 0), (0, LANE - N_EXP)), constant_values=-1e30).reshape(n_layer, 1, LANE)
    rope_cos, rope_sin = rope_tables(s_lat)
    cache_k4 = cache_k.reshape(n_lat, n_layer, past, DA_W)
    cache_v4 = cache_v.reshape(n_lat, n_layer, past, DA_W)

    ks, vs, rgs, cs, ns, ms = [], [], [], [], [], []
    for l in range(n_layer):
        lam_init = 0.8 - 0.6 * math.exp(-0.3 * l)
        z_rg, z, kv_ctx, gates = in_projection(x, mod, n1g, w_in_bf, w_gate_bf, rope_cos, rope_sin, l, geo)

        rg_h0 = jnp.concatenate([jnp.zeros((n_ctx, 2, RG_W), F32), state_rglru[:, l]], axis=0).reshape(geo.n_seq, 2, 1, RG_W)
        hb, rg_sb = rglru_direction(z_rg, None, rg_conv_w, rg_cb, rg_gate_dense, rg_gate_bias, rg_lam, rg_h0, l, geo, True)
        out_rg, rg_sf = rglru_direction(z_rg, hb, rg_conv_w, rg_cb, rg_gate_dense, rg_gate_bias, rg_lam, rg_h0, l, geo, False)

        da_ctx = attention_context(z, da_lambda, da_ng, l, lam_init, geo)
        da_lat = attention_latent(z, cache_k4, cache_v4, da_lambda, da_ng, l, lam_init, geo)

        c0 = jnp.concatenate([jnp.zeros((n_ctx, 2, ML_H, ML_DH, ML_DH), F32), state_mlstm_C[:, l]], axis=0)
        n0 = jnp.concatenate([jnp.zeros((n_ctx, 2, ML_H, ML_DH), F32), state_mlstm_n[:, l]], axis=0)
        n0 = n0.reshape(geo.n_seq, 2, ML_H, 1, ML_DH)
        m0 = jnp.concatenate([jnp.zeros((n_ctx, 2, ML_H), F32), state_mlstm_m[:, l]], axis=0)
        m0 = jnp.broadcast_to(m0[..., None, None], (geo.n_seq, 2, ML_H, 1, ML_DH))
        mhb, cb, nb, mb = mlstm_direction(z, None, gates, ml_gb, ml_ng, c0, n0, m0, l, geo, True)
        out_ml, cf, nf, mf = mlstm_direction(z, mhb, gates, ml_gb, ml_ng, c0, n0, m0, l, geo, False)

        x1, h2_packed, top_e, top_w = out_projection(x, out_rg, da_ctx, da_lat, out_ml, w_out_bf, mod, n2g,
                                                     router_w_pad, router_b_pad, l, geo)
        dest, blk_e, n_used = moe_routing(top_e[:, :TOP_K])
        x_sorted = moe_dispatch(dest, h2_packed, moe_num_blocks(geo.t * TOP_K) * TM_MOE)
        yb = moe_experts(x_sorted, blk_e, n_used, moe_w_gu, moe_b_gu, moe_w_down, moe_b_down, l)
        x = moe_combine(dest, x1, top_w, yb, mod, l, geo)

        ks.append(kv_ctx[:, 0:DA_W].reshape(n_ctx, s_ctx, DA_H, DA_DV))
        vs.append(kv_ctx[:, DA_W:2 * DA_W].reshape(n_ctx, s_ctx, DA_H, DA_DV))
        rgs.append(jnp.stack([rg_sf[:n_ctx, 0], rg_sb[:n_ctx, 0]], axis=1))
        cs.append(jnp.stack([cf[:n_ctx], cb[:n_ctx]], axis=1))
        ns.append(jnp.stack([nf[:n_ctx, :, 0], nb[:n_ctx, :, 0]], axis=1))
        ms.append(jnp.stack([mf[:n_ctx, :, 0, 0], mb[:n_ctx, :, 0, 0]], axis=1))

    y = final_norm(x, final_g)
    y_prompt = y[:geo.t_ctx].reshape(n_ctx, s_ctx, D)
    y_sample = y[geo.t_ctx:].reshape(n_lat, s_lat, D)
    return (y_prompt, y_sample, jnp.stack(ks, axis=1), jnp.stack(vs, axis=1), jnp.stack(rgs, axis=1),
            jnp.stack(cs, axis=1), jnp.stack(ns, axis=1), jnp.stack(ms, axis=1))
```

```python
import functools
import math

import jax
import jax.numpy as jnp
from jax import lax
from jax.experimental import pallas as pl
from jax.experimental.pallas import tpu as pltpu

F32 = jnp.float32
BF16 = jnp.bfloat16
HIGHEST = lax.Precision.HIGHEST

D = 2048
EPS = 1e-6
RG_W = D // 4
RG_BLOCKS = 8
RG_C = 8.0
DA_W = D // 2
DA_H = 8
DA_DV = 128
DA_DK = 64
ML_W = D // 4
ML_H = 4
ML_DH = 128
N_EXP = 32
TOP_K = 4
FF = D
SWIGLU_LIMIT = 7.0
SWIGLU_ALPHA = 1.702
GRID_W = 64
ROPE_THETA = 10000.0
N_MAIN = 2 * RG_W + 3 * DA_W + 4 * ML_W
N_GATE = 4 * ML_H

LANE = 128
SUBLANE = 8

TM_IN = 1024
TN_IN = 1024
NORM_ROWS = 256
TM_OUT = 512
CHUNK = 256
TQ = 1024
KV_STEP = 512
KV_CHUNKS = (1536, 1024, 512)
TM_MOE = 1024
TF_MOE = 512
TN_MOE = 512
TM_DSP = 256
TM_EW = 512
TM_CMB = 256
VMEM_LIMIT = 60 * 1024 * 1024
LOG2E = 1.4426950408889634


def _cparams(sem, **kw):
    return pltpu.CompilerParams(dimension_semantics=sem, vmem_limit_bytes=VMEM_LIMIT, **kw)


def _rms(x, eps=EPS):
    return x * lax.rsqrt(jnp.mean(x * x, axis=-1, keepdims=True) + eps)


def _log_sigmoid(x):
    return jnp.minimum(x, 0.0) - jnp.log1p(jnp.exp(-jnp.abs(x)))


class Geo:
    def __init__(self, n_ctx, s_ctx, n_lat, s_lat):
        self.n_ctx, self.s_ctx, self.n_lat, self.s_lat = n_ctx, s_ctx, n_lat, s_lat
        self.t_ctx = n_ctx * s_ctx
        self.t_lat = n_lat * s_lat
        self.t = self.t_ctx + self.t_lat
        self.n_seq = n_ctx + n_lat
        self.cond_rows = -(-(n_lat + 1) // SUBLANE) * SUBLANE
        self.ctx_cond_row = n_lat


def _ada_kernel(c_ref, w_ref, b_ref, o_ref):
    c = c_ref[...]
    s = (c * jax.nn.sigmoid(c)).astype(BF16)
    o_ref[...] = jnp.dot(s, w_ref[...].astype(BF16), preferred_element_type=F32) + b_ref[...]


def ada_modulation(cond, ada_w, ada_b):
    n_layer, _, n6 = ada_w.shape
    rows = cond.shape[0]
    tn = 1536
    return pl.pallas_call(
        _ada_kernel,
        grid=(n_layer, n6 // tn),
        in_specs=[pl.BlockSpec((rows, D), lambda l, j: (0, 0)),
                  pl.BlockSpec((None, D, tn), lambda l, j: (l, 0, j)),
                  pl.BlockSpec((None, 1, tn), lambda l, j: (l, 0, j))],
        out_specs=pl.BlockSpec((None, rows, tn), lambda l, j: (l, 0, j)),
        out_shape=jax.ShapeDtypeStruct((n_layer, rows, n6), F32),
        compiler_params=_cparams(("arbitrary", "arbitrary")),
        name="ada_modulation",
    )(cond, ada_w, ada_b.reshape(n_layer, 1, n6))


def _cond_row(i, geo, tm):
    n_ctx_blk = geo.t_ctx // tm
    return jnp.where(i < n_ctx_blk, geo.ctx_cond_row, (i - n_ctx_blk) // (geo.s_lat // tm))


def _inproj_kernel(x_ref, mod_ref, g_ref, w_ref, wg_ref, cos_ref, sin_ref,
                   zrg_ref, z_ref, kv_ref, gt_ref, h_scr, *, geo):
    i = pl.program_id(0)
    j = pl.program_id(1)

    @pl.when(j == 0)
    def _():
        r = _cond_row(i, geo, TM_IN)
        gain = g_ref[...] * (1.0 + mod_ref[pl.ds(r, 1), D:2 * D])
        shift = mod_ref[pl.ds(r, 1), 0:D]
        for r0 in range(0, TM_IN, NORM_ROWS):
            rows = slice(r0, r0 + NORM_ROWS)
            h_scr[rows, :] = (_rms(x_ref[rows, :]) * gain + shift).astype(BF16)
        gt_ref[...] = jnp.dot(h_scr[...], wg_ref[...], preferred_element_type=F32)

    q_tile0 = (2 * RG_W) // TN_IN
    k_tile = q_tile0 + DA_W // TN_IN
    is_qk = jnp.logical_and(j >= q_tile0, j < q_tile0 + (2 * DA_W) // TN_IN)
    is_kv = jnp.logical_and(j >= k_tile, j < k_tile + (2 * DA_W) // TN_IN)

    for n0 in range(0, TN_IN, TN_IN // 2):
        cols = slice(n0, n0 + TN_IN // 2)
        z = jnp.dot(h_scr[...], w_ref[:, cols], preferred_element_type=F32)

        @pl.when(j == 0)
        def _():
            zrg_ref[:, cols] = z

        @pl.when(is_qk)
        def _():
            qs = jnp.where(j == q_tile0, DA_DK ** -0.5 * LOG2E, 1.0)
            c = cos_ref[...] * qs
            s = sin_ref[...] * qs
            lane = lax.broadcasted_iota(jnp.int32, (TM_IN, LANE), 1)
            first = (lane % (DA_DK // 2)) < (DA_DK // 4)
            for k in range(TN_IN // 2 // LANE):
                zk = z[:, k * LANE:(k + 1) * LANE]
                partner = jnp.where(first, pltpu.roll(zk, LANE - DA_DK // 4, 1), pltpu.roll(zk, DA_DK // 4, 1))
                z_ref[:, n0 + k * LANE:n0 + (k + 1) * LANE] = (zk * c + partner * s).astype(BF16)

        @pl.when(jnp.logical_and(j > 0, jnp.logical_not(is_qk)))
        def _():
            z_ref[:, cols] = z.astype(BF16)

        @pl.when(jnp.logical_and(is_kv, i < geo.t_ctx // TM_IN))
        def _():
            kv_ref[:, cols] = z


def in_projection(x, mod, norm_g, w_in_bf, w_gate_bf, rope_cos, rope_sin, layer, geo):
    assert TN_IN == 2 * RG_W == DA_W
    n_i = geo.t // TM_IN
    n_j = N_MAIN // TN_IN
    n_ctx_blk = geo.t_ctx // TM_IN
    lat_blk = geo.s_lat // TM_IN

    def rope_idx(i, j):
        return (jnp.where(i < n_ctx_blk, 0, 1 + (i - n_ctx_blk) % lat_blk), 0)

    def kv_idx(i, j):
        k_tile = (2 * RG_W + DA_W) // TN_IN
        return (jnp.minimum(i, n_ctx_blk - 1), jnp.where(i < n_ctx_blk, jnp.clip(j - k_tile, 0, 1), 1))

    return pl.pallas_call(
        functools.partial(_inproj_kernel, geo=geo),
        grid=(n_i, n_j),
        in_specs=[pl.BlockSpec((TM_IN, D), lambda i, j: (i, 0)),
                  pl.BlockSpec((None, geo.cond_rows, 6 * D), lambda i, j: (layer, 0, 0)),
                  pl.BlockSpec((None, 1, D), lambda i, j: (layer, 0, 0)),
                  pl.BlockSpec((None, D, TN_IN), lambda i, j: (layer, 0, j)),
                  pl.BlockSpec((None, D, N_GATE), lambda i, j: (layer, 0, 0)),
                  pl.BlockSpec((TM_IN, LANE), rope_idx),
                  pl.BlockSpec((TM_IN, LANE), rope_idx)],
        out_specs=[pl.BlockSpec((TM_IN, TN_IN), lambda i, j: (i, 0)),
                   pl.BlockSpec((TM_IN, TN_IN), lambda i, j: (i, jnp.maximum(j - 1, 0))),
                   pl.BlockSpec((TM_IN, TN_IN), kv_idx),
                   pl.BlockSpec((TM_IN, N_GATE), lambda i, j: (i, 0))],
        out_shape=[jax.ShapeDtypeStruct((geo.t, TN_IN), F32),
                   jax.ShapeDtypeStruct((geo.t, N_MAIN - TN_IN), BF16),
                   jax.ShapeDtypeStruct((geo.t_ctx, 2 * DA_W), F32),
                   jax.ShapeDtypeStruct((geo.t, N_GATE), F32)],
        scratch_shapes=[pltpu.VMEM((TM_IN, D), BF16)],
        compiler_params=_cparams(("arbitrary", "arbitrary")),
        name="in_projection",
    )(x, mod, norm_g, w_in_bf, w_gate_bf, rope_cos, rope_sin)


def rope_tables(s_lat):
    t = jnp.arange(s_lat)
    row = (t // GRID_W).astype(F32)
    col = (t % GRID_W).astype(F32)
    d = jnp.arange(LANE) % DA_DK
    axis = d // (DA_DK // 2)
    n_freq = DA_DK // 4
    inv = ROPE_THETA ** (-(d % n_freq).astype(F32) / n_freq)
    second = (d % (DA_DK // 2)) >= n_freq
    pos = jnp.where(axis[None, :] == 0, row[:, None], col[:, None])
    ang = pos * inv[None, :]
    cos = jnp.cos(ang)
    sin = jnp.sin(ang) * jnp.where(second, 1.0, -1.0)[None, :]
    cos = jnp.concatenate([jnp.ones((TM_IN, LANE), F32), cos], axis=0)
    sin = jnp.concatenate([jnp.zeros((TM_IN, LANE), F32), sin], axis=0)
    return cos, sin


def _chunk_info(g, geo):
    ncc = geo.t_ctx // CHUNK
    cps_c = geo.s_ctx // CHUNK
    cps_l = geo.s_lat // CHUNK
    is_ctx = g < ncc
    sid = jnp.where(is_ctx, g // cps_c, geo.n_ctx + (g - ncc) // cps_l)
    pos = jnp.where(is_ctx, g % cps_c, (g - ncc) % cps_l)
    last = jnp.where(is_ctx, cps_c - 1, cps_l - 1)
    return sid, pos == 0, pos == last


def _rglru_kernel(*refs, geo, rev, n_chunk):
    if rev:
        (xp_ref, x_ref, xn_ref, cw_ref, cb_ref, wg_ref, bg_ref, lam_ref, h0_ref,
         h_ref, st_ref, pad_scr, a_scr, u_scr, carry_scr) = refs
    else:
        (xp_ref, x_ref, xn_ref, y_ref, hb_ref, cw_ref, cb_ref, wg_ref, bg_ref, lam_ref, h0_ref,
         h_ref, st_ref, pad_scr, a_scr, u_scr, carry_scr) = refs
    p = pl.program_id(0)
    g = n_chunk - 1 - p if rev else p
    _, t_first, t_last = _chunk_info(g, geo)
    starts = t_last if rev else t_first

    pad_scr[0:SUBLANE, :] = jnp.where(t_first, 0.0, xp_ref[...])
    pad_scr[SUBLANE:SUBLANE + CHUNK, :] = x_ref[...]
    pad_scr[SUBLANE + CHUNK:2 * SUBLANE + CHUNK, :] = jnp.where(t_last, 0.0, xn_ref[...])
    xc = cb_ref[...]
    for tap in range(4):
        xc = xc + cw_ref[tap:tap + 1, :] * pad_scr[pl.ds(SUBLANE - 2 + tap, CHUNK), :]

    gates = jnp.dot(xc.astype(BF16), wg_ref[...], preferred_element_type=F32) + bg_ref[...]
    r = jax.nn.sigmoid(gates[:, :RG_W])
    i = jax.nn.sigmoid(gates[:, RG_W:])
    log_a = (RG_C * _log_sigmoid(lam_ref[...])) * r
    a = jnp.exp(log_a)
    u = jnp.sqrt(-jnp.tanh(log_a) * (a * a + 1.0)) * (i * xc)

    row = lax.broadcasted_iota(jnp.int32, (CHUNK, RG_W), 0) % SUBLANE
    s = 1
    while s < SUBLANE:
        shift = CHUNK - s if rev else s
        valid = (row < SUBLANE - s) if rev else (row >= s)
        a_sh = pltpu.roll(a, shift, 0)
        u_sh = pltpu.roll(u, shift, 0)
        u = jnp.where(valid, u + a * u_sh, u)
        a = jnp.where(valid, a * a_sh, a)
        s *= 2
    a_scr[...] = a
    u_scr[...] = u

    @pl.when(starts)
    def _():
        carry_scr[...] = h0_ref[...]

    n_grp = CHUNK // SUBLANE

    def body(k, carry):
        grp = n_grp - 1 - k if rev else k
        rows = pl.ds(pl.multiple_of(grp * SUBLANE, SUBLANE), SUBLANE)
        h8 = u_scr[rows, :] + a_scr[rows, :] * carry
        u_scr[rows, :] = h8
        return h8[0:1, :] if rev else h8[SUBLANE - 1:SUBLANE, :]

    carry = lax.fori_loop(0, n_grp, body, carry_scr[...])
    carry_scr[...] = carry
    st_ref[...] = carry
    if rev:
        h_ref[...] = u_scr[...]
    else:
        h_ref[...] = (jax.nn.gelu(y_ref[...]) * (u_scr[...] + hb_ref[...])).astype(BF16)


def rglru_direction(z, hb, conv_w, conv_b, gate_w_bf, gate_b, lam, h0, layer, geo, rev):
    n_chunk = geo.t // CHUNK
    d = 1 if rev else 0
    cpb = CHUNK // SUBLANE
    n_blk8 = geo.t // SUBLANE

    def gi(p):
        return n_chunk - 1 - p if rev else p

    def sid(p):
        return _chunk_info(gi(p), geo)[0]

    x_specs = [pl.BlockSpec((SUBLANE, RG_W), lambda p: (jnp.maximum(gi(p) * cpb - 1, 0), 0)),
               pl.BlockSpec((CHUNK, RG_W), lambda p: (gi(p), 0)),
               pl.BlockSpec((SUBLANE, RG_W), lambda p: (jnp.minimum((gi(p) + 1) * cpb, n_blk8 - 1), 0))]
    w_specs = [pl.BlockSpec((None, 4, RG_W), lambda p: (layer, 0, 0)),
               pl.BlockSpec((None, 1, RG_W), lambda p: (layer, 0, 0)),
               pl.BlockSpec((None, None, RG_W, 2 * RG_W), lambda p: (layer, d, 0, 0)),
               pl.BlockSpec((None, None, 1, 2 * RG_W), lambda p: (layer, d, 0, 0)),
               pl.BlockSpec((None, None, 1, RG_W), lambda p: (layer, d, 0, 0)),
               pl.BlockSpec((None, None, 1, RG_W), lambda p: (sid(p), d, 0, 0))]
    if rev:
        in_specs = x_specs + w_specs
        args = (z, z, z, conv_w, conv_b, gate_w_bf, gate_b, lam, h0)
        h_dtype = F32
    else:
        in_specs = x_specs + [pl.BlockSpec((CHUNK, RG_W), lambda p: (gi(p), 1)),
                              pl.BlockSpec((CHUNK, RG_W), lambda p: (gi(p), 0))] + w_specs
        args = (z, z, z, z, hb, conv_w, conv_b, gate_w_bf, gate_b, lam, h0)
        h_dtype = BF16
    return pl.pallas_call(
        functools.partial(_rglru_kernel, geo=geo, rev=rev, n_chunk=n_chunk),
        grid=(n_chunk,),
        in_specs=in_specs,
        out_specs=[pl.BlockSpec((CHUNK, RG_W), lambda p: (gi(p), 0)),
                   pl.BlockSpec((None, 1, RG_W), lambda p: (sid(p), 0, 0))],
        out_shape=[jax.ShapeDtypeStruct((geo.t, RG_W), h_dtype),
                   jax.ShapeDtypeStruct((geo.n_seq, 1, RG_W), F32)],
        scratch_shapes=[pltpu.VMEM((CHUNK + 2 * SUBLANE, RG_W), F32),
                        pltpu.VMEM((CHUNK, RG_W), F32),
                        pltpu.VMEM((CHUNK, RG_W), F32),
                        pltpu.VMEM((1, RG_W), F32)],
        compiler_params=_cparams(("arbitrary",)),
        name="rglru_bwd" if rev else "rglru_fwd",
    )(*args)


def _attn_kernel(*refs, past, n_new, kv_chunk, lam_init):
    if past:
        q_ref, kc_ref, vc_ref, k_ref, v_ref, lam_ref, g_ref, o_ref, kt_scr, v_scr = refs
    else:
        q_ref, k_ref, v_ref, lam_ref, g_ref, o_ref, kt_scr, v_scr = refs
    n_kv = past + n_new
    step = min(KV_STEP, n_new)

    @pl.when(pl.program_id(2) == 0)
    def _():
        for r0 in range(0, past, step):
            rows = slice(r0, r0 + step)
            kt_scr[:, rows] = kc_ref[rows, :].T.astype(BF16)
            v_scr[rows, 0:DA_DV] = vc_ref[rows, :].astype(BF16)
        for r0 in range(0, n_new, step):
            src = slice(r0, r0 + step)
            dst = slice(past + r0, past + r0 + step)
            kt_scr[:, dst] = k_ref[src, :].astype(F32).T.astype(BF16)
            v_scr[dst, 0:DA_DV] = v_ref[src, :]
        v_scr[:, DA_DV:2 * DA_DV] = jnp.ones((n_kv, DA_DV), BF16)

    q = q_ref[...].astype(F32)
    lane = lax.broadcasted_iota(jnp.int32, q.shape, 1)
    heads = []
    for c in range(2):
        in_comp = (lane < DA_DK) if c == 0 else (lane >= DA_DK)
        qc = jnp.where(in_comp, q, 0.0).astype(BF16)
        m = None
        acc = None
        for c0 in range(0, n_kv, kv_chunk):
            cols = slice(c0, c0 + kv_chunk)
            s = jnp.dot(qc, kt_scr[:, cols], preferred_element_type=F32)
            mx = jnp.max(s, axis=-1, keepdims=True)
            m_new = mx if m is None else jnp.maximum(m, mx)
            pv = jnp.dot(jnp.exp2(s - m_new).astype(BF16), v_scr[cols, :], preferred_element_type=F32)
            acc = pv if m is None else jnp.exp2(m - m_new) * acc + pv
            m = m_new
        heads.append(acc[:, 0:DA_DV] / acc[:, DA_DV:2 * DA_DV])

    lq = lam_ref[...]
    lam = (jnp.exp(jnp.sum(lq[0:1] * lq[1:2], axis=-1, keepdims=True))
           - jnp.exp(jnp.sum(lq[2:3] * lq[3:4], axis=-1, keepdims=True)) + lam_init)
    o = heads[0] - lam * heads[1]
    o_ref[...] = (_rms(o) * g_ref[...] * (1.0 - lam_init)).astype(BF16)


def _kv_chunk(n_kv):
    for cand in KV_CHUNKS:
        if n_kv % cand == 0:
            return cand
    return n_kv


def _attn_scratch(n_kv):
    return [pltpu.VMEM((2 * DA_DK, n_kv), BF16), pltpu.VMEM((n_kv, 2 * DA_DV), BF16)]


_Q_COL = 0
_K_COL = _Q_COL + DA_W // LANE
_V_COL = _K_COL + DA_W // LANE


def attention_context(z, da_lambda, da_norm_g, layer, lam_init, geo):
    s = geo.s_ctx
    return pl.pallas_call(
        functools.partial(_attn_kernel, past=0, n_new=s, kv_chunk=_kv_chunk(s), lam_init=lam_init),
        grid=(geo.n_ctx, DA_H, 1),
        in_specs=[pl.BlockSpec((s, LANE), lambda b, h, qi: (b, _Q_COL + h)),
                  pl.BlockSpec((s, LANE), lambda b, h, qi: (b, _K_COL + h)),
                  pl.BlockSpec((s, LANE), lambda b, h, qi: (b, _V_COL + h)),
                  pl.BlockSpec((None, 4, DA_DK), lambda b, h, qi: (layer, 0, 0)),
                  pl.BlockSpec((None, 1, DA_DV), lambda b, h, qi: (layer, 0, 0))],
        out_specs=pl.BlockSpec((s, LANE), lambda b, h, qi: (b, h)),
        out_shape=jax.ShapeDtypeStruct((geo.t_ctx, DA_W), BF16),
        scratch_shapes=_attn_scratch(s),
        compiler_params=_cparams(("arbitrary",) * 3),
        name="attention_context",
    )(z, z, z, da_lambda, da_norm_g)


def attention_latent(z, cache_k, cache_v, da_lambda, da_norm_g, layer, lam_init, geo):
    past = cache_k.shape[2]
    s = geo.s_lat
    n_kv = past + s
    nq = s // TQ
    q0 = geo.t_ctx // TQ
    k0 = geo.t_ctx // s
    return pl.pallas_call(
        functools.partial(_attn_kernel, past=past, n_new=s, kv_chunk=_kv_chunk(n_kv), lam_init=lam_init),
        grid=(geo.n_lat, DA_H, nq),
        in_specs=[pl.BlockSpec((TQ, LANE), lambda b, h, qi: (q0 + b * nq + qi, _Q_COL + h)),
                  pl.BlockSpec((None, None, past, LANE), lambda b, h, qi: (b, layer, 0, h)),
                  pl.BlockSpec((None, None, past, LANE), lambda b, h, qi: (b, layer, 0, h)),
                  pl.BlockSpec((s, LANE), lambda b, h, qi: (k0 + b, _K_COL + h)),
                  pl.BlockSpec((s, LANE), lambda b, h, qi: (k0 + b, _V_COL + h)),
                  pl.BlockSpec((None, 4, DA_DK), lambda b, h, qi: (layer, 0, 0)),
                  pl.BlockSpec((None, 1, DA_DV), lambda b, h, qi: (layer, 0, 0))],
        out_specs=pl.BlockSpec((TQ, LANE), lambda b, h, qi: (b * nq + qi, h)),
        out_shape=jax.ShapeDtypeStruct((geo.t_lat, DA_W), BF16),
        scratch_shapes=_attn_scratch(n_kv),
        compiler_params=_cparams(("arbitrary",) * 3),
        name="attention_latent",
    )(z, cache_k, cache_v, z, z, da_lambda, da_norm_g)


def _mlstm_kernel(*refs, geo, rev, n_chunk):
    if rev:
        (q_ref, k_ref, v_ref, gt_ref, gb_ref, c0_ref, n0_ref, m0_ref,
         h_ref, co_ref, no_ref, mo_ref, c_scr, n_scr, m_scr) = refs
    else:
        (q_ref, k_ref, v_ref, o_ref, hb_ref, gt_ref, gb_ref, ng_ref, c0_ref, n0_ref, m0_ref,
         h_ref, co_ref, no_ref, mo_ref, c_scr, n_scr, m_scr) = refs
    p = pl.program_id(0)
    g = n_chunk - 1 - p if rev else p
    _, t_first, t_last = _chunk_info(g, geo)
    starts = t_last if rev else t_first
    d = 1 if rev else 0

    @pl.when(starts)
    def _():
        c_scr[...] = c0_ref[...]
        n_scr[...] = n0_ref[...]
        m_scr[...] = m0_ref[...]

    gt = gt_ref[...] + gb_ref[...]
    eye = jnp.where(lax.broadcasted_iota(jnp.int32, (N_GATE, N_GATE), 0)
                    == lax.broadcasted_iota(jnp.int32, (N_GATE, N_GATE), 1), 1.0, 0.0)
    gtt = lax.dot_general(eye, gt, (((1,), (1,)), ((), ())), precision=HIGHEST, preferred_element_type=F32)
    ti = lax.broadcasted_iota(jnp.int32, (CHUNK, CHUNK), 0)
    si = lax.broadcasted_iota(jnp.int32, (CHUNK, CHUNK), 1)
    keep = (si >= ti) if rev else (si <= ti)
    tri = jnp.where(keep, 1.0, 0.0)
    b_col_all = jnp.dot(tri, _log_sigmoid(gt), precision=HIGHEST, preferred_element_type=F32)
    tri_t = jnp.where((ti >= si) if rev else (ti <= si), 1.0, 0.0)
    b_row_all = jnp.dot(_log_sigmoid(gtt), tri_t, precision=HIGHEST, preferred_element_type=F32)
    edge = 0 if rev else CHUNK - 1

    for h in range(ML_H):
        ci = d * 2 * ML_H + h
        cf = ci + ML_H
        hs = slice(h * ML_DH, (h + 1) * ML_DH)
        qb = q_ref[:, hs]
        vb = v_ref[:, hs]
        qh = qb.astype(F32)
        kh = k_ref[:, hs].astype(F32) * (ML_DH ** -0.5)
        vh = vb.astype(F32)
        kb = kh.astype(BF16)
        li_col = gt[:, ci:ci + 1]
        li_row = gtt[ci:ci + 1, :]
        b_col = b_col_all[:, cf:cf + 1]
        b_row = b_row_all[cf:cf + 1, :]
        m_prev = m_scr[h][:, 0:1]
        c_prev = c_scr[h]
        n_prev = n_scr[h]

        dm = jnp.where(keep, b_col - b_row + li_row, -jnp.inf)
        inter = b_col + m_prev
        m_t = jnp.maximum(inter, jnp.max(dm, axis=-1, keepdims=True))
        s = lax.dot_general(qb, kb, (((1,), (1,)), ((), ())), preferred_element_type=F32) * jnp.exp(dm - m_t)
        w_inter = jnp.exp(inter - m_t)
        cq = lax.dot_general(qb, c_prev.astype(BF16), (((1,), (1,)), ((), ())), preferred_element_type=F32)
        num = jnp.dot(s.astype(BF16), vb, preferred_element_type=F32) + w_inter * cq
        den = jnp.sum(s, axis=-1, keepdims=True) + w_inter * jnp.sum(qh * n_prev, axis=-1, keepdims=True)
        hh = num / jnp.maximum(jnp.abs(den), jnp.exp(-m_t))

        b_tot = b_col[edge:edge + 1, :]
        gg = b_tot - b_col + li_col
        m_new = jnp.maximum(b_tot + m_prev, jnp.max(gg, axis=0, keepdims=True))
        wgt = jnp.exp(gg - m_new)
        decay = jnp.exp(b_tot + m_prev - m_new)
        wv_t = (wgt * vh).T.astype(BF16)
        c_new = decay * c_prev + jnp.dot(wv_t, kb, preferred_element_type=F32)
        n_new = decay * n_prev + jnp.sum(wgt * kh, axis=0, keepdims=True)
        c_scr[h] = c_new
        n_scr[h] = n_new
        m_scr[h] = jnp.broadcast_to(m_new, (1, ML_DH))
        co_ref[h] = c_new
        no_ref[h] = n_new
        mo_ref[h] = jnp.broadcast_to(m_new, (1, ML_DH))

        if rev:
            h_ref[:, hs] = hh
        else:
            gated = jax.nn.sigmoid(o_ref[:, hs].astype(F32)) * (hh + hb_ref[:, hs])
            h_ref[:, hs] = (_rms(gated) * ng_ref[:, hs]).astype(BF16)


_MLQ_COL = (3 * DA_W) // ML_W


def mlstm_direction(z, hb, gates, gate_b, norm_g, c0, n0, m0, layer, geo, rev):
    n_chunk = geo.t // CHUNK
    d = 1 if rev else 0

    def gi(p):
        return n_chunk - 1 - p if rev else p

    def sid(p):
        return _chunk_info(gi(p), geo)[0]

    def zspec(col):
        return pl.BlockSpec((CHUNK, ML_W), lambda p: (gi(p), col))

    qkv = [zspec(_MLQ_COL), zspec(_MLQ_COL + 1), zspec(_MLQ_COL + 2)]
    gate_specs = [pl.BlockSpec((CHUNK, N_GATE), lambda p: (gi(p), 0)),
                  pl.BlockSpec((None, 1, N_GATE), lambda p: (layer, 0, 0))]
    st_specs = [pl.BlockSpec((None, None, ML_H, ML_DH, ML_DH), lambda p: (sid(p), d, 0, 0, 0)),
                pl.BlockSpec((None, None, ML_H, 1, ML_DH), lambda p: (sid(p), d, 0, 0, 0)),
                pl.BlockSpec((None, None, ML_H, 1, ML_DH), lambda p: (sid(p), d, 0, 0, 0))]
    if rev:
        in_specs = qkv + gate_specs + st_specs
        args = (z, z, z, gates, gate_b, c0, n0, m0)
        h_dtype = F32
    else:
        in_specs = (qkv + [zspec(_MLQ_COL + 3), pl.BlockSpec((CHUNK, ML_W), lambda p: (gi(p), 0))] + gate_specs
                    + [pl.BlockSpec((None, 1, ML_W), lambda p: (layer, 0, 0))] + st_specs)
        args = (z, z, z, z, hb, gates, gate_b, norm_g, c0, n0, m0)
        h_dtype = BF16
    return pl.pallas_call(
        functools.partial(_mlstm_kernel, geo=geo, rev=rev, n_chunk=n_chunk),
        grid=(n_chunk,),
        in_specs=in_specs,
        out_specs=[pl.BlockSpec((CHUNK, ML_W), lambda p: (gi(p), 0)),
                   pl.BlockSpec((None, ML_H, ML_DH, ML_DH), lambda p: (sid(p), 0, 0, 0)),
                   pl.BlockSpec((None, ML_H, 1, ML_DH), lambda p: (sid(p), 0, 0, 0)),
                   pl.BlockSpec((None, ML_H, 1, ML_DH), lambda p: (sid(p), 0, 0, 0))],
        out_shape=[jax.ShapeDtypeStruct((geo.t, ML_W), h_dtype),
                   jax.ShapeDtypeStruct((geo.n_seq, ML_H, ML_DH, ML_DH), F32),
                   jax.ShapeDtypeStruct((geo.n_seq, ML_H, 1, ML_DH), F32),
                   jax.ShapeDtypeStruct((geo.n_seq, ML_H, 1, ML_DH), F32)],
        scratch_shapes=[pltpu.VMEM((ML_H, ML_DH, ML_DH), F32),
                        pltpu.VMEM((ML_H, 1, ML_DH), F32),
                        pltpu.VMEM((ML_H, 1, ML_DH), F32)],
        compiler_params=_cparams(("arbitrary",)),
        name="mlstm_bwd" if rev else "mlstm_fwd",
    )(*args)


def _outproj_kernel(x_ref, rg_ref, dac_ref, dal_ref, ml_ref, w_ref, mod_ref, g_ref, wr_ref, br_ref,
                    x1_ref, h2_ref, te_ref, tw_ref, *, geo):
    i = pl.program_id(0)
    r = _cond_row(i, geo, TM_OUT)
    gate1 = mod_ref[pl.ds(r, 1), 2 * D:3 * D]
    shift2 = mod_ref[pl.ds(r, 1), 3 * D:4 * D]
    scale2 = mod_ref[pl.ds(r, 1), 4 * D:5 * D]
    da = jnp.where(i < geo.t_ctx // TM_OUT, dac_ref[...], dal_ref[...])
    y = jnp.dot(rg_ref[...], w_ref[0:RG_W, :], preferred_element_type=F32)
    y = y + jnp.dot(da, w_ref[RG_W:RG_W + DA_W, :], preferred_element_type=F32)
    y = y + jnp.dot(ml_ref[...], w_ref[RG_W + DA_W:D, :], preferred_element_type=F32)
    x1 = x_ref[...] + gate1 * y
    x1_ref[...] = x1
    h2 = _rms(x1) * g_ref[...] * (1.0 + scale2) + shift2
    lo = lax.bitcast_convert_type(h2[:, 0:D // 2].astype(BF16).astype(F32), jnp.uint32) >> 16
    hi = lax.bitcast_convert_type(h2[:, D // 2:D].astype(BF16).astype(F32), jnp.uint32)
    h2_ref[...] = hi | lo

    logits = jnp.dot(h2.astype(BF16), wr_ref[...], preferred_element_type=F32) + br_ref[...]
    lane = lax.broadcasted_iota(jnp.int32, logits.shape, 1)
    lane_f = lane.astype(F32)
    sel_e = jnp.zeros(logits.shape, jnp.int32)
    sel_v = jnp.zeros(logits.shape, F32)
    top0 = None
    total = None
    for k in range(TOP_K):
        mk = jnp.max(logits, axis=-1, keepdims=True)
        idx = jnp.min(jnp.where(logits == mk, lane_f, float(LANE)), axis=-1, keepdims=True)
        if k == 0:
            top0 = mk
        ek = jnp.exp(mk - top0)
        total = ek if total is None else total + ek
        sel_e = jnp.where(lane == k, idx.astype(jnp.int32), sel_e)
        sel_v = jnp.where(lane == k, ek, sel_v)
        logits = jnp.where(lane_f == idx, -jnp.inf, logits)
    te_ref[...] = sel_e
    tw_ref[...] = sel_v / total


def out_projection(x, out_rg, da_ctx, da_lat, out_ml, w_out_bf, mod, norm_g, router_w, router_b, layer, geo):
    n_i = geo.t // TM_OUT
    n_c = geo.t_ctx // TM_OUT
    return pl.pallas_call(
        functools.partial(_outproj_kernel, geo=geo),
        grid=(n_i,),
        in_specs=[pl.BlockSpec((TM_OUT, D), lambda i: (i, 0)),
                  pl.BlockSpec((TM_OUT, RG_W), lambda i: (i, 0)),
                  pl.BlockSpec((TM_OUT, DA_W), lambda i: (jnp.minimum(i, n_c - 1), 0)),
                  pl.BlockSpec((TM_OUT, DA_W), lambda i: (jnp.maximum(i - n_c, 0), 0)),
                  pl.BlockSpec((TM_OUT, ML_W), lambda i: (i, 0)),
                  pl.BlockSpec((None, D, D), lambda i: (layer, 0, 0), pipeline_mode=pl.Buffered(1)),
                  pl.BlockSpec((None, geo.cond_rows, 6 * D), lambda i: (layer, 0, 0), pipeline_mode=pl.Buffered(1)),
                  pl.BlockSpec((None, 1, D), lambda i: (layer, 0, 0)),
                  pl.BlockSpec((None, D, LANE), lambda i: (layer, 0, 0), pipeline_mode=pl.Buffered(1)),
                  pl.BlockSpec((None, 1, LANE), lambda i: (layer, 0, 0))],
        out_specs=[pl.BlockSpec((TM_OUT, D), lambda i: (i, 0)),
                   pl.BlockSpec((TM_OUT, D // 2), lambda i: (i, 0)),
                   pl.BlockSpec((TM_OUT, LANE), lambda i: (i, 0)),
                   pl.BlockSpec((TM_OUT, LANE), lambda i: (i, 0))],
        out_shape=[jax.ShapeDtypeStruct((geo.t, D), F32),
                   jax.ShapeDtypeStruct((geo.t, D // 2), jnp.uint32),
                   jax.ShapeDtypeStruct((geo.t, LANE), jnp.int32),
                   jax.ShapeDtypeStruct((geo.t, LANE), F32)],
        compiler_params=_cparams(("arbitrary",)),
        name="out_projection",
    )(x, out_rg, da_ctx, da_lat, out_ml, w_out_bf, mod, norm_g, router_w, router_b)


def _dispatch_kernel(dest_ref, h_ref, xs_in, xs_out, sem):
    del xs_in

    def body(r, carry):
        for k in range(TOP_K):
            d = dest_ref[r * TOP_K + k]
            pltpu.make_async_copy(h_ref.at[pl.ds(r, 1)], xs_out.at[pl.ds(d, 1)], sem).start()
        return carry

    lax.fori_loop(0, TM_DSP, body, 0)
    for k in range(TOP_K):
        pltpu.make_async_copy(h_ref, xs_out.at[pl.ds(0, TM_DSP)], sem).wait()


def moe_dispatch(dest, h2_packed, n_rows):
    t = h2_packed.shape[0]
    return pl.pallas_call(
        _dispatch_kernel,
        grid=(t // TM_DSP,),
        in_specs=[pl.BlockSpec((TM_DSP * TOP_K,), lambda i: (i,), memory_space=pltpu.SMEM),
                  pl.BlockSpec((TM_DSP, D // 2), lambda i: (i, 0)),
                  pl.BlockSpec(memory_space=pl.ANY)],
        out_specs=pl.BlockSpec(memory_space=pl.ANY),
        out_shape=jax.ShapeDtypeStruct((n_rows, D // 2), jnp.uint32),
        scratch_shapes=[pltpu.SemaphoreType.DMA(())],
        input_output_aliases={2: 0},
        compiler_params=_cparams(("arbitrary",), disable_bounds_checks=True),
        name="moe_dispatch",
    )(dest, h2_packed, jnp.zeros((n_rows, D // 2), jnp.uint32))


def _pack_bf16_pairs(lo_f32, hi_f32):
    lo = lax.bitcast_convert_type(lo_f32.astype(BF16).astype(F32), jnp.uint32) >> 16
    hi = lax.bitcast_convert_type(hi_f32.astype(BF16).astype(F32), jnp.uint32)
    return hi | lo


def _unpack_bf16_pairs(u):
    lo = lax.bitcast_convert_type(u << 16, F32)
    hi = lax.bitcast_convert_type((u >> 16) << 16, F32)
    return lo, hi


def _moe_kernel(be_ref, nu_ref, x_ref, bgu_ref, bdn_ref, wgu_hbm, wdn_hbm, o_ref, x_scr, act_scr, wbuf, sem, *, layer):
    m = pl.program_id(0)
    n_used = nu_ref[0]
    used = m < n_used
    half = D // 2
    n_a = FF // TF_MOE
    n_b = D // TN_MOE
    n_chunk = n_a + n_b

    def chunk_copies(blk, c, slot):
        e = be_ref[blk]
        if c < n_a:
            cols = pl.ds(c * TF_MOE, TF_MOE)
            ucols = pl.ds(FF + c * TF_MOE, TF_MOE)
            return [pltpu.make_async_copy(wgu_hbm.at[layer, e, :, cols], wbuf.at[slot, 0], sem.at[slot]),
                    pltpu.make_async_copy(wgu_hbm.at[layer, e, :, ucols], wbuf.at[slot, 1], sem.at[slot])]
        cols = pl.ds((c - n_a) * TN_MOE, TN_MOE)
        return [pltpu.make_async_copy(wdn_hbm.at[layer, e, :, cols], wbuf.at[slot, 0], sem.at[slot])]

    @pl.when(jnp.logical_not(used))
    def _():
        o_ref[...] = jnp.zeros(o_ref.shape, jnp.uint32)

    @pl.when(used)
    def _():
        @pl.when(m == 0)
        def _():
            for cp in chunk_copies(m, 0, 0):
                cp.start()

        lo, hi = _unpack_bf16_pairs(x_ref[...])
        x_scr[:, 0:half] = lo.astype(BF16)
        x_scr[:, half:D] = hi.astype(BF16)

        for c in range(n_chunk):
            slot = c % 2
            if c + 1 < n_chunk:
                for cp in chunk_copies(m, c + 1, (c + 1) % 2):
                    cp.start()
            else:
                @pl.when(m + 1 < n_used)
                def _():
                    for cp in chunk_copies(m + 1, 0, 0):
                        cp.start()
            for cp in chunk_copies(m, c, slot):
                cp.wait()

            if c < n_a:
                cols = slice(c * TF_MOE, (c + 1) * TF_MOE)
                x = x_scr[...]
                gate = jnp.dot(x, wbuf[slot, 0].astype(BF16), preferred_element_type=F32) + bgu_ref[:, cols]
                up = (jnp.dot(x, wbuf[slot, 1].astype(BF16), preferred_element_type=F32)
                      + bgu_ref[:, FF + c * TF_MOE:FF + (c + 1) * TF_MOE])
                gate = jnp.minimum(gate, SWIGLU_LIMIT)
                up = jnp.clip(up, -SWIGLU_LIMIT, SWIGLU_LIMIT)
                act_scr[:, cols] = (gate * jax.nn.sigmoid(SWIGLU_ALPHA * gate) * (up + 1.0)).astype(BF16)
            else:
                n = c - n_a
                y = (jnp.dot(act_scr[...], wbuf[slot, 0].astype(BF16), preferred_element_type=F32)
                     + bdn_ref[:, n * TN_MOE:(n + 1) * TN_MOE])
                o_ref[:, n * (TN_MOE // 2):(n + 1) * (TN_MOE // 2)] = _pack_bf16_pairs(
                    y[:, 0:TN_MOE // 2], y[:, TN_MOE // 2:TN_MOE])


def moe_experts(x_sorted, blk_e, n_used, w_gu, b_gu, w_dn, b_dn, layer):
    assert TF_MOE == TN_MOE and FF == D and ((FF // TF_MOE) + (D // TN_MOE)) % 2 == 0
    n_rows = x_sorted.shape[0]
    n_blk = n_rows // TM_MOE
    grid_spec = pltpu.PrefetchScalarGridSpec(
        num_scalar_prefetch=2,
        grid=(n_blk,),
        in_specs=[pl.BlockSpec((TM_MOE, D // 2), lambda m, be, nu: (m, 0)),
                  pl.BlockSpec((None, None, 1, 2 * FF), lambda m, be, nu: (layer, be[m], 0, 0)),
                  pl.BlockSpec((None, None, 1, D), lambda m, be, nu: (layer, be[m], 0, 0)),
                  pl.BlockSpec(memory_space=pl.ANY),
                  pl.BlockSpec(memory_space=pl.ANY)],
        out_specs=pl.BlockSpec((TM_MOE, D // 2), lambda m, be, nu: (m, 0)),
        scratch_shapes=[pltpu.VMEM((TM_MOE, D), BF16), pltpu.VMEM((TM_MOE, FF), BF16),
                        pltpu.VMEM((2, 2, D, TF_MOE), F32), pltpu.SemaphoreType.DMA((2,))],
    )
    n_layer, n_exp = w_gu.shape[:2]
    return pl.pallas_call(
        functools.partial(_moe_kernel, layer=layer),
        grid_spec=grid_spec,
        out_shape=jax.ShapeDtypeStruct((n_rows, D // 2), jnp.uint32),
        compiler_params=_cparams(("arbitrary",)),
        name="moe_experts",
    )(blk_e, n_used, x_sorted, b_gu.reshape(n_layer, n_exp, 1, 2 * FF), b_dn.reshape(n_layer, n_exp, 1, D),
      w_gu, w_dn)


def moe_routing(top_e):
    flat_e = top_e.reshape(-1)
    n_assign = flat_e.shape[0]
    onehot = (flat_e[:, None] == jnp.arange(N_EXP, dtype=jnp.int32)[None, :]).astype(jnp.int32)
    csum = jnp.cumsum(onehot, axis=0)
    counts = csum[-1]
    pos = jnp.take_along_axis(csum, flat_e[:, None], axis=1)[:, 0] - 1
    padded = (counts + TM_MOE - 1) // TM_MOE * TM_MOE
    pad_end = jnp.cumsum(padded)
    pad_start = pad_end - padded
    dest = pad_start[flat_e] + pos
    n_blk = moe_num_blocks(n_assign)
    blk_e = jnp.minimum(jnp.searchsorted(pad_end, jnp.arange(n_blk, dtype=jnp.int32) * TM_MOE, side='right'),
                        N_EXP - 1).astype(jnp.int32)
    n_used = (pad_end[-1] // TM_MOE).astype(jnp.int32).reshape(1)
    return dest.astype(jnp.int32), blk_e, n_used


def moe_num_blocks(n_assign):
    return -(-n_assign // TM_MOE) + N_EXP


def _combine_kernel(dfirst_ref, dnext_ref, x_ref, tw_ref, mod_ref, yb_ref, o_ref, buf, sem, *, geo, n_blk):
    i = pl.program_id(0)

    def gather(d_ref, slot):
        def body(r, carry):
            for k in range(TOP_K):
                d = d_ref[r * TOP_K + k]
                pltpu.make_async_copy(yb_ref.at[pl.ds(d, 1)], buf.at[slot, k, pl.ds(r, 1)], sem.at[slot]).start()
            return carry

        lax.fori_loop(0, TM_CMB, body, 0)

    @pl.when(i == 0)
    def _():
        gather(dfirst_ref, 0)

    @pl.when(i + 1 < n_blk)
    def _():
        gather(dnext_ref, (i + 1) % 2)

    slot = i % 2
    for k in range(TOP_K):
        pltpu.make_async_copy(yb_ref.at[pl.ds(0, TM_CMB)], buf.at[slot, k], sem.at[slot]).wait()

    r = _cond_row(i, geo, TM_CMB)
    tw = tw_ref[...]
    hw = TN_MOE // 2
    for n in range(D // TN_MOE):
        y_lo = None
        y_hi = None
        for k in range(TOP_K):
            lo, hi = _unpack_bf16_pairs(buf[slot, k, :, n * hw:(n + 1) * hw])
            w = tw[:, k:k + 1]
            y_lo = w * lo if y_lo is None else y_lo + w * lo
            y_hi = w * hi if y_hi is None else y_hi + w * hi
        c0 = n * TN_MOE
        o_ref[:, c0:c0 + hw] = x_ref[:, c0:c0 + hw] + mod_ref[pl.ds(r, 1), 5 * D + c0:5 * D + c0 + hw] * y_lo
        o_ref[:, c0 + hw:c0 + 2 * hw] = (x_ref[:, c0 + hw:c0 + 2 * hw]
                                         + mod_ref[pl.ds(r, 1), 5 * D + c0 + hw:5 * D + c0 + 2 * hw] * y_hi)


def moe_combine(dest, x1, top_w, yb, mod, layer, geo):
    n_blk = geo.t // TM_CMB
    dblk = TM_CMB * TOP_K
    return pl.pallas_call(
        functools.partial(_combine_kernel, geo=geo, n_blk=n_blk),
        grid=(n_blk,),
        in_specs=[pl.BlockSpec((dblk,), lambda i: (0,), memory_space=pltpu.SMEM),
                  pl.BlockSpec((dblk,), lambda i: (jnp.minimum(i + 1, n_blk - 1),), memory_space=pltpu.SMEM),
                  pl.BlockSpec((TM_CMB, D), lambda i: (i, 0)),
                  pl.BlockSpec((TM_CMB, LANE), lambda i: (i, 0)),
                  pl.BlockSpec((None, geo.cond_rows, 6 * D), lambda i: (layer, 0, 0)),
                  pl.BlockSpec(memory_space=pl.ANY)],
        out_specs=pl.BlockSpec((TM_CMB, D), lambda i: (i, 0)),
        out_shape=jax.ShapeDtypeStruct((geo.t, D), F32),
        scratch_shapes=[pltpu.VMEM((2, TOP_K, TM_CMB, D // 2), jnp.uint32), pltpu.SemaphoreType.DMA((2,))],
        compiler_params=_cparams(("arbitrary",), disable_bounds_checks=True),
        name="moe_combine",
    )(dest, dest, x1, top_w, mod, yb)


def _final_norm_kernel(x_ref, g_ref, o_ref):
    o_ref[...] = _rms(x_ref[...]) * g_ref[...]


def final_norm(x, g):
    t = x.shape[0]
    return pl.pallas_call(
        _final_norm_kernel,
        grid=(t // TM_EW,),
        in_specs=[pl.BlockSpec((TM_EW, D), lambda i: (i, 0)), pl.BlockSpec((1, D), lambda i: (0, 0))],
        out_specs=pl.BlockSpec((TM_EW, D), lambda i: (i, 0)),
        out_shape=jax.ShapeDtypeStruct((t, D), F32),
        compiler_params=_cparams(("arbitrary",)),
        name="final_norm",
    )(x, g.reshape(1, D))


def _block_diag(w):
    eye = jnp.eye(RG_BLOCKS, dtype=w.dtype)
    dense = w[..., :, :, None, :] * eye[:, None, :, None]
    return dense.reshape(*w.shape[:-3], RG_W, RG_W)


def kernel(x_prompt, x_sample, c, cache_k, cache_v, state_rglru, state_mlstm_C, state_mlstm_n, state_mlstm_m, c_ctx, ada_w, ada_b, norm1_g, norm2_g, w_in, rg_conv_w, rg_conv_b, rg_gate_w, rg_gate_b, rg_lambda, da_lambda, da_norm_g, ml_gate_b, ml_norm_g, w_out, router_w, router_b, moe_w_gu, moe_b_gu, moe_w_down, moe_b_down, final_g):
    n_ctx, s_ctx, _ = x_prompt.shape
    n_lat, s_lat, _ = x_sample.shape
    n_layer = ada_w.shape[0]
    past = cache_k.shape[2]
    geo = Geo(n_ctx, s_ctx, n_lat, s_lat)
    assert geo.t_ctx % TM_IN == 0 and s_lat % TM_IN == 0 and s_ctx % CHUNK == 0 and s_lat % CHUNK == 0
    assert s_lat % TQ == 0 and geo.t_ctx % s_lat == 0 and past % KV_STEP == 0 and s_ctx % SUBLANE == 0

    x = jnp.concatenate([x_prompt.reshape(geo.t_ctx, D), x_sample.reshape(geo.t_lat, D)], axis=0)
    cond = jnp.zeros((geo.cond_rows, D), F32).at[:n_lat].set(c).at[n_lat].set(c_ctx)
    mod = ada_modulation(cond, ada_w, ada_b)

    w_in_bf = w_in.astype(BF16)
    w_gate_bf = w_in_bf[:, :, N_MAIN:]
    w_out_bf = w_out.astype(BF16)
    rg_gate_dense = jnp.concatenate([_block_diag(rg_gate_w[:, :, 0]), _block_diag(rg_gate_w[:, :, 1])],
                                    axis=-1).astype(BF16)
    rg_gate_bias = rg_gate_b.reshape(n_layer, 2, 1, 2 * RG_W)
    rg_lam = rg_lambda.reshape(n_layer, 2, 1, RG_W)
    rg_cb = rg_conv_b.reshape(n_layer, 1, RG_W)
    ml_gb = ml_gate_b.reshape(n_layer, 1, N_GATE)
    ml_ng = ml_norm_g.reshape(n_layer, 1, ML_W)
    da_ng = da_norm_g.reshape(n_layer, 1, DA_DV)
    n1g = norm1_g.reshape(n_layer, 1, D)
    n2g = norm2_g.reshape(n_layer, 1, D)
    router_w_pad = jnp.pad(router_w, ((0, 0), (0, 0), (0, LANE - N_EXP))).astype(BF16)
    router_b_pad = jnp.pad(router_b, ((0, 0), (0, LANE - N_EXP)), constant_values=-1e30).reshape(n_layer, 1, LANE)
    rope_cos, rope_sin = rope_tables(s_lat)
    cache_k4 = cache_k.reshape(n_lat, n_layer, past, DA_W)
    cache_v4 = cache_v.reshape(n_lat, n_layer, past, DA_W)

    ks, vs, rgs, cs, ns, ms = [], [], [], [], [], []
    for l in range(n_layer):
        lam_init = 0.8 - 0.6 * math.exp(-0.3 * l)
        z_rg, z, kv_ctx, gates = in_projection(x, mod, n1g, w_in_bf, w_gate_bf, rope_cos, rope_sin, l, geo)

        rg_h0 = jnp.concatenate([jnp.zeros((n_ctx, 2, RG_W), F32), state_rglru[:, l]], axis=0).reshape(geo.n_seq, 2, 1, RG_W)
        hb, rg_sb = rglru_direction(z_rg, None, rg_conv_w, rg_cb, rg_gate_dense, rg_gate_bias, rg_lam, rg_h0, l, geo, True)
        out_rg, rg_sf = rglru_direction(z_rg, hb, rg_conv_w, rg_cb, rg_gate_dense, rg_gate_bias, rg_lam, rg_h0, l, geo, False)

        da_ctx = attention_context(z, da_lambda, da_ng, l, lam_init, geo)
        da_lat = attention_latent(z, cache_k4, cache_v4, da_lambda, da_ng, l, lam_init, geo)

        c0 = jnp.concatenate([jnp.zeros((n_ctx, 2, ML_H, ML_DH, ML_DH), F32), state_mlstm_C[:, l]], axis=0)
        n0 = jnp.concatenate([jnp.zeros((n_ctx, 2, ML_H, ML_DH), F32), state_mlstm_n[:, l]], axis=0)
        n0 = n0.reshape(geo.n_seq, 2, ML_H, 1, ML_DH)
        m0 = jnp.concatenate([jnp.zeros((n_ctx, 2, ML_H), F32), state_mlstm_m[:, l]], axis=0)
        m0 = jnp.broadcast_to(m0[..., None, None], (geo.n_seq, 2, ML_H, 1, ML_DH))
        mhb, cb, nb, mb = mlstm_direction(z, None, gates, ml_gb, ml_ng, c0, n0, m0, l, geo, True)
        out_ml, cf, nf, mf = mlstm_direction(z, mhb, gates, ml_gb, ml_ng, c0, n0, m0, l, geo, False)

        x1, h2_packed, top_e, top_w = out_projection(x, out_rg, da_ctx, da_lat, out_ml, w_out_bf, mod, n2g,
                                                     router_w_pad, router_b_pad, l, geo)
        dest, blk_e, n_used = moe_routing(top_e[:, :TOP_K])
        x_sorted = moe_dispatch(dest, h2_packed, moe_num_blocks(geo.t * TOP_K) * TM_MOE)
        yb = moe_experts(x_sorted, blk_e, n_used, moe_w_gu, moe_b_gu, moe_w_down, moe_b_down, l)
        x = moe_combine(dest, x1, top_w, yb, mod, l, geo)

        ks.append(kv_ctx[:, 0:DA_W].reshape(n_ctx, s_ctx, DA_H, DA_DV))
        vs.append(kv_ctx[:, DA_W:2 * DA_W].reshape(n_ctx, s_ctx, DA_H, DA_DV))
        rgs.append(jnp.stack([rg_sf[:n_ctx, 0], rg_sb[:n_ctx, 0]], axis=1))
        cs.append(jnp.stack([cf[:n_ctx], cb[:n_ctx]], axis=1))
        ns.append(jnp.stack([nf[:n_ctx, :, 0], nb[:n_ctx, :, 0]], axis=1))
        ms.append(jnp.stack([mf[:n_ctx, :, 0, 0], mb[:n_ctx, :, 0, 0]], axis=1))

    y = final_norm(x, final_g)
    y_prompt = y[:geo.t_ctx].reshape(n_ctx, s_ctx, D)
    y_sample = y[geo.t_ctx:].reshape(n_lat, s_lat, D)
    return (y_prompt, y_sample, jnp.stack(ks, axis=1), jnp.stack(vs, axis=1), jnp.stack(rgs, axis=1),
            jnp.stack(cs, axis=1), jnp.stack(ns, axis=1), jnp.stack(ms, axis=1))
```

```python
import functools
import math

import jax
import jax.numpy as jnp
from jax import lax
from jax.experimental import pallas as pl
from jax.experimental.pallas import tpu as pltpu

F32 = jnp.float32
BF16 = jnp.bfloat16
HIGHEST = lax.Precision.HIGHEST

D = 2048
EPS = 1e-6
RG_W = D // 4
RG_BLOCKS = 8
RG_C = 8.0
DA_W = D // 2
DA_H = 8
DA_DV = 128
DA_DK = 64
ML_W = D // 4
ML_H = 4
ML_DH = 128
N_EXP = 32
TOP_K = 4
FF = D
SWIGLU_LIMIT = 7.0
SWIGLU_ALPHA = 1.702
GRID_W = 64
ROPE_THETA = 10000.0
N_MAIN = 2 * RG_W + 3 * DA_W + 4 * ML_W
N_GATE = 4 * ML_H

LANE = 128
SUBLANE = 8

TM_IN = 1024
TN_IN = 1024
NORM_ROWS = 256
TM_OUT = 512
CHUNK = 256
TQ = 1024
KV_STEP = 512
KV_CHUNKS = (768, 512)
TM_MOE = 1024
TF_MOE = 512
TN_MOE = 512
TM_DSP = 256
TM_EW = 512
TM_CMB = 256
VMEM_LIMIT = 60 * 1024 * 1024
LOG2E = 1.4426950408889634


def _cparams(sem, **kw):
    return pltpu.CompilerParams(dimension_semantics=sem, vmem_limit_bytes=VMEM_LIMIT, **kw)


def _rms(x, eps=EPS):
    return x * lax.rsqrt(jnp.mean(x * x, axis=-1, keepdims=True) + eps)


def _log_sigmoid(x):
    return jnp.minimum(x, 0.0) - jnp.log1p(jnp.exp(-jnp.abs(x)))


class Geo:
    def __init__(self, n_ctx, s_ctx, n_lat, s_lat):
        self.n_ctx, self.s_ctx, self.n_lat, self.s_lat = n_ctx, s_ctx, n_lat, s_lat
        self.t_ctx = n_ctx * s_ctx
        self.t_lat = n_lat * s_lat
        self.t = self.t_ctx + self.t_lat
        self.n_seq = n_ctx + n_lat
        self.cond_rows = -(-(n_lat + 1) // SUBLANE) * SUBLANE
        self.ctx_cond_row = n_lat


def _ada_kernel(c_ref, w_ref, b_ref, o_ref):
    c = c_ref[...]
    s = (c * jax.nn.sigmoid(c)).astype(BF16)
    o_ref[...] = jnp.dot(s, w_ref[...].astype(BF16), preferred_element_type=F32) + b_ref[...]


def ada_modulation(cond, ada_w, ada_b):
    n_layer, _, n6 = ada_w.shape
    rows = cond.shape[0]
    tn = 1536
    return pl.pallas_call(
        _ada_kernel,
        grid=(n_layer, n6 // tn),
        in_specs=[pl.BlockSpec((rows, D), lambda l, j: (0, 0)),
                  pl.BlockSpec((None, D, tn), lambda l, j: (l, 0, j)),
                  pl.BlockSpec((None, 1, tn), lambda l, j: (l, 0, j))],
        out_specs=pl.BlockSpec((None, rows, tn), lambda l, j: (l, 0, j)),
        out_shape=jax.ShapeDtypeStruct((n_layer, rows, n6), F32),
        compiler_params=_cparams(("arbitrary", "arbitrary")),
        name="ada_modulation",
    )(cond, ada_w, ada_b.reshape(n_layer, 1, n6))


def _cond_row(i, geo, tm):
    n_ctx_blk = geo.t_ctx // tm
    return jnp.where(i < n_ctx_blk, geo.ctx_cond_row, (i - n_ctx_blk) // (geo.s_lat // tm))


def _inproj_kernel(x_ref, mod_ref, g_ref, w_ref, wg_ref, cos_ref, sin_ref,
                   zrg_ref, z_ref, kv_ref, gt_ref, h_scr, *, geo):
    i = pl.program_id(0)
    j = pl.program_id(1)

    @pl.when(j == 0)
    def _():
        r = _cond_row(i, geo, TM_IN)
        gain = g_ref[...] * (1.0 + mod_ref[pl.ds(r, 1), D:2 * D])
        shift = mod_ref[pl.ds(r, 1), 0:D]
        for r0 in range(0, TM_IN, NORM_ROWS):
            rows = slice(r0, r0 + NORM_ROWS)
            h_scr[rows, :] = (_rms(x_ref[rows, :]) * gain + shift).astype(BF16)
        gt_ref[...] = jnp.dot(h_scr[...], wg_ref[...], preferred_element_type=F32)

    q_tile0 = (2 * RG_W) // TN_IN
    k_tile = q_tile0 + DA_W // TN_IN
    is_qk = jnp.logical_and(j >= q_tile0, j < q_tile0 + (2 * DA_W) // TN_IN)
    is_kv = jnp.logical_and(j >= k_tile, j < k_tile + (2 * DA_W) // TN_IN)

    for n0 in range(0, TN_IN, TN_IN // 2):
        cols = slice(n0, n0 + TN_IN // 2)
        z = jnp.dot(h_scr[...], w_ref[:, cols], preferred_element_type=F32)

        @pl.when(j == 0)
        def _():
            zrg_ref[:, cols] = z

        @pl.when(is_qk)
        def _():
            qs = jnp.where(j == q_tile0, DA_DK ** -0.5 * LOG2E, 1.0)
            c = cos_ref[...] * qs
            s = sin_ref[...] * qs
            lane = lax.broadcasted_iota(jnp.int32, (TM_IN, LANE), 1)
            first = (lane % (DA_DK // 2)) < (DA_DK // 4)
            for k in range(TN_IN // 2 // LANE):
                zk = z[:, k * LANE:(k + 1) * LANE]
                partner = jnp.where(first, pltpu.roll(zk, LANE - DA_DK // 4, 1), pltpu.roll(zk, DA_DK // 4, 1))
                z_ref[:, n0 + k * LANE:n0 + (k + 1) * LANE] = (zk * c + partner * s).astype(BF16)

        @pl.when(jnp.logical_and(j > 0, jnp.logical_not(is_qk)))
        def _():
            z_ref[:, cols] = z.astype(BF16)

        @pl.when(jnp.logical_and(is_kv, i < geo.t_ctx // TM_IN))
        def _():
            kv_ref[:, cols] = z


def in_projection(x, mod, norm_g, w_in_bf, w_gate_bf, rope_cos, rope_sin, layer, geo):
    assert TN_IN == 2 * RG_W == DA_W
    n_i = geo.t // TM_IN
    n_j = N_MAIN // TN_IN
    n_ctx_blk = geo.t_ctx // TM_IN
    lat_blk = geo.s_lat // TM_IN

    def rope_idx(i, j):
        return (jnp.where(i < n_ctx_blk, 0, 1 + (i - n_ctx_blk) % lat_blk), 0)

    def kv_idx(i, j):
        k_tile = (2 * RG_W + DA_W) // TN_IN
        return (jnp.minimum(i, n_ctx_blk - 1), jnp.where(i < n_ctx_blk, jnp.clip(j - k_tile, 0, 1), 1))

    return pl.pallas_call(
        functools.partial(_inproj_kernel, geo=geo),
        grid=(n_i, n_j),
        in_specs=[pl.BlockSpec((TM_IN, D), lambda i, j: (i, 0)),
                  pl.BlockSpec((None, geo.cond_rows, 6 * D), lambda i, j: (layer, 0, 0)),
                  pl.BlockSpec((None, 1, D), lambda i, j: (layer, 0, 0)),
                  pl.BlockSpec((None, D, TN_IN), lambda i, j: (layer, 0, j)),
                  pl.BlockSpec((None, D, N_GATE), lambda i, j: (layer, 0, 0)),
                  pl.BlockSpec((TM_IN, LANE), rope_idx),
                  pl.BlockSpec((TM_IN, LANE), rope_idx)],
        out_specs=[pl.BlockSpec((TM_IN, TN_IN), lambda i, j: (i, 0)),
                   pl.BlockSpec((TM_IN, TN_IN), lambda i, j: (i, jnp.maximum(j - 1, 0))),
                   pl.BlockSpec((TM_IN, TN_IN), kv_idx),
                   pl.BlockSpec((TM_IN, N_GATE), lambda i, j: (i, 0))],
        out_shape=[jax.ShapeDtypeStruct((geo.t, TN_IN), F32),
                   jax.ShapeDtypeStruct((geo.t, N_MAIN - TN_IN), BF16),
                   jax.ShapeDtypeStruct((geo.t_ctx, 2 * DA_W), F32),
                   jax.ShapeDtypeStruct((geo.t, N_GATE), F32)],
        scratch_shapes=[pltpu.VMEM((TM_IN, D), BF16)],
        compiler_params=_cparams(("arbitrary", "arbitrary")),
        name="in_projection",
    )(x, mod, norm_g, w_in_bf, w_gate_bf, rope_cos, rope_sin)


def rope_tables(s_lat):
    t = jnp.arange(s_lat)
    row = (t // GRID_W).astype(F32)
    col = (t % GRID_W).astype(F32)
    d = jnp.arange(LANE) % DA_DK
    axis = d // (DA_DK // 2)
    n_freq = DA_DK // 4
    inv = ROPE_THETA ** (-(d % n_freq).astype(F32) / n_freq)
    second = (d % (DA_DK // 2)) >= n_freq
    pos = jnp.where(axis[None, :] == 0, row[:, None], col[:, None])
    ang = pos * inv[None, :]
    cos = jnp.cos(ang)
    sin = jnp.sin(ang) * jnp.where(second, 1.0, -1.0)[None, :]
    cos = jnp.concatenate([jnp.ones((TM_IN, LANE), F32), cos], axis=0)
    sin = jnp.concatenate([jnp.zeros((TM_IN, LANE), F32), sin], axis=0)
    return cos, sin


def _chunk_info(g, geo):
    ncc = geo.t_ctx // CHUNK
    cps_c = geo.s_ctx // CHUNK
    cps_l = geo.s_lat // CHUNK
    is_ctx = g < ncc
    sid = jnp.where(is_ctx, g // cps_c, geo.n_ctx + (g - ncc) // cps_l)
    pos = jnp.where(is_ctx, g % cps_c, (g - ncc) % cps_l)
    last = jnp.where(is_ctx, cps_c - 1, cps_l - 1)
    return sid, pos == 0, pos == last


def _rglru_kernel(*refs, geo, rev, n_chunk):
    if rev:
        (xp_ref, x_ref, xn_ref, cw_ref, cb_ref, wg_ref, bg_ref, lam_ref, h0_ref,
         h_ref, st_ref, pad_scr, a_scr, u_scr, carry_scr) = refs
    else:
        (xp_ref, x_ref, xn_ref, y_ref, hb_ref, cw_ref, cb_ref, wg_ref, bg_ref, lam_ref, h0_ref,
         h_ref, st_ref, pad_scr, a_scr, u_scr, carry_scr) = refs
    p = pl.program_id(0)
    g = n_chunk - 1 - p if rev else p
    _, t_first, t_last = _chunk_info(g, geo)
    starts = t_last if rev else t_first

    pad_scr[0:SUBLANE, :] = jnp.where(t_first, 0.0, xp_ref[...])
    pad_scr[SUBLANE:SUBLANE + CHUNK, :] = x_ref[...]
    pad_scr[SUBLANE + CHUNK:2 * SUBLANE + CHUNK, :] = jnp.where(t_last, 0.0, xn_ref[...])
    xc = cb_ref[...]
    for tap in range(4):
        xc = xc + cw_ref[tap:tap + 1, :] * pad_scr[pl.ds(SUBLANE - 2 + tap, CHUNK), :]

    gates = jnp.dot(xc.astype(BF16), wg_ref[...], preferred_element_type=F32) + bg_ref[...]
    r = jax.nn.sigmoid(gates[:, :RG_W])
    i = jax.nn.sigmoid(gates[:, RG_W:])
    log_a = (RG_C * _log_sigmoid(lam_ref[...])) * r
    a = jnp.exp(log_a)
    u = jnp.sqrt(-jnp.tanh(log_a) * (a * a + 1.0)) * (i * xc)

    row = lax.broadcasted_iota(jnp.int32, (CHUNK, RG_W), 0) % SUBLANE
    s = 1
    while s < SUBLANE:
        shift = CHUNK - s if rev else s
        valid = (row < SUBLANE - s) if rev else (row >= s)
        a_sh = pltpu.roll(a, shift, 0)
        u_sh = pltpu.roll(u, shift, 0)
        u = jnp.where(valid, u + a * u_sh, u)
        a = jnp.where(valid, a * a_sh, a)
        s *= 2
    a_scr[...] = a
    u_scr[...] = u

    @pl.when(starts)
    def _():
        carry_scr[...] = h0_ref[...]

    n_grp = CHUNK // SUBLANE

    def body(k, carry):
        grp = n_grp - 1 - k if rev else k
        rows = pl.ds(pl.multiple_of(grp * SUBLANE, SUBLANE), SUBLANE)
        h8 = u_scr[rows, :] + a_scr[rows, :] * carry
        u_scr[rows, :] = h8
        return h8[0:1, :] if rev else h8[SUBLANE - 1:SUBLANE, :]

    carry = lax.fori_loop(0, n_grp, body, carry_scr[...])
    carry_scr[...] = carry
    st_ref[...] = carry
    if rev:
        h_ref[...] = u_scr[...]
    else:
        h_ref[...] = (jax.nn.gelu(y_ref[...]) * (u_scr[...] + hb_ref[...])).astype(BF16)


def rglru_direction(z, hb, conv_w, conv_b, gate_w_bf, gate_b, lam, h0, layer, geo, rev):
    n_chunk = geo.t // CHUNK
    d = 1 if rev else 0
    cpb = CHUNK // SUBLANE
    n_blk8 = geo.t // SUBLANE

    def gi(p):
        return n_chunk - 1 - p if rev else p

    def sid(p):
        return _chunk_info(gi(p), geo)[0]

    x_specs = [pl.BlockSpec((SUBLANE, RG_W), lambda p: (jnp.maximum(gi(p) * cpb - 1, 0), 0)),
               pl.BlockSpec((CHUNK, RG_W), lambda p: (gi(p), 0)),
               pl.BlockSpec((SUBLANE, RG_W), lambda p: (jnp.minimum((gi(p) + 1) * cpb, n_blk8 - 1), 0))]
    w_specs = [pl.BlockSpec((None, 4, RG_W), lambda p: (layer, 0, 0)),
               pl.BlockSpec((None, 1, RG_W), lambda p: (layer, 0, 0)),
               pl.BlockSpec((None, None, RG_W, 2 * RG_W), lambda p: (layer, d, 0, 0)),
               pl.BlockSpec((None, None, 1, 2 * RG_W), lambda p: (layer, d, 0, 0)),
               pl.BlockSpec((None, None, 1, RG_W), lambda p: (layer, d, 0, 0)),
               pl.BlockSpec((None, None, 1, RG_W), lambda p: (sid(p), d, 0, 0))]
    if rev:
        in_specs = x_specs + w_specs
        args = (z, z, z, conv_w, conv_b, gate_w_bf, gate_b, lam, h0)
        h_dtype = F32
    else:
        in_specs = x_specs + [pl.BlockSpec((CHUNK, RG_W), lambda p: (gi(p), 1)),
                              pl.BlockSpec((CHUNK, RG_W), lambda p: (gi(p), 0))] + w_specs
        args = (z, z, z, z, hb, conv_w, conv_b, gate_w_bf, gate_b, lam, h0)
        h_dtype = BF16
    return pl.pallas_call(
        functools.partial(_rglru_kernel, geo=geo, rev=rev, n_chunk=n_chunk),
        grid=(n_chunk,),
        in_specs=in_specs,
        out_specs=[pl.BlockSpec((CHUNK, RG_W), lambda p: (gi(p), 0)),
                   pl.BlockSpec((None, 1, RG_W), lambda p: (sid(p), 0, 0))],
        out_shape=[jax.ShapeDtypeStruct((geo.t, RG_W), h_dtype),
                   jax.ShapeDtypeStruct((geo.n_seq, 1, RG_W), F32)],
        scratch_shapes=[pltpu.VMEM((CHUNK + 2 * SUBLANE, RG_W), F32),
                        pltpu.VMEM((CHUNK, RG_W), F32),
                        pltpu.VMEM((CHUNK, RG_W), F32),
                        pltpu.VMEM((1, RG_W), F32)],
        compiler_params=_cparams(("arbitrary",)),
        name="rglru_bwd" if rev else "rglru_fwd",
    )(*args)


def _attn_kernel(*refs, past, n_new, kv_chunk, lam_init):
    if past:
        q_ref, kc_ref, vc_ref, k_ref, v_ref, lam_ref, g_ref, o_ref, kt_scr, v_scr = refs
    else:
        q_ref, k_ref, v_ref, lam_ref, g_ref, o_ref, kt_scr, v_scr = refs
    n_kv = past + n_new
    step = min(KV_STEP, n_new)

    @pl.when(pl.program_id(2) == 0)
    def _():
        for r0 in range(0, past, step):
            rows = slice(r0, r0 + step)
            kt_scr[:, rows] = kc_ref[rows, :].T.astype(BF16)
            v_scr[rows, 0:DA_DV] = vc_ref[rows, :].astype(BF16)
        for r0 in range(0, n_new, step):
            src = slice(r0, r0 + step)
            dst = slice(past + r0, past + r0 + step)
            kt_scr[:, dst] = k_ref[src, :].astype(F32).T.astype(BF16)
            v_scr[dst, 0:DA_DV] = v_ref[src, :]
        v_scr[:, DA_DV:2 * DA_DV] = jnp.ones((n_kv, DA_DV), BF16)

    q = q_ref[...].astype(F32)
    lane = lax.broadcasted_iota(jnp.int32, q.shape, 1)
    heads = []
    for c in range(2):
        in_comp = (lane < DA_DK) if c == 0 else (lane >= DA_DK)
        qc = jnp.where(in_comp, q, 0.0).astype(BF16)
        m = None
        acc = None
        for c0 in range(0, n_kv, kv_chunk):
            cols = slice(c0, c0 + kv_chunk)
            s = jnp.dot(qc, kt_scr[:, cols], preferred_element_type=F32)
            mx = jnp.max(s, axis=-1, keepdims=True)
            m_new = mx if m is None else jnp.maximum(m, mx)
            pv = jnp.dot(jnp.exp2(s - m_new).astype(BF16), v_scr[cols, :], preferred_element_type=F32)
            acc = pv if m is None else jnp.exp2(m - m_new) * acc + pv
            m = m_new
        heads.append(acc[:, 0:DA_DV] / acc[:, DA_DV:2 * DA_DV])

    lq = lam_ref[...]
    lam = (jnp.exp(jnp.sum(lq[0:1] * lq[1:2], axis=-1, keepdims=True))
           - jnp.exp(jnp.sum(lq[2:3] * lq[3:4], axis=-1, keepdims=True)) + lam_init)
    o = heads[0] - lam * heads[1]
    o_ref[...] = (_rms(o) * g_ref[...] * (1.0 - lam_init)).astype(BF16)


def _kv_chunk(n_kv):
    for cand in KV_CHUNKS:
        if n_kv % cand == 0:
            return cand
    return n_kv


def _attn_scratch(n_kv):
    return [pltpu.VMEM((2 * DA_DK, n_kv), BF16), pltpu.VMEM((n_kv, 2 * DA_DV), BF16)]


_Q_COL = 0
_K_COL = _Q_COL + DA_W // LANE
_V_COL = _K_COL + DA_W // LANE


def attention_context(z, da_lambda, da_norm_g, layer, lam_init, geo):
    s = geo.s_ctx
    return pl.pallas_call(
        functools.partial(_attn_kernel, past=0, n_new=s, kv_chunk=_kv_chunk(s), lam_init=lam_init),
        grid=(geo.n_ctx, DA_H, 1),
        in_specs=[pl.BlockSpec((s, LANE), lambda b, h, qi: (b, _Q_COL + h)),
                  pl.BlockSpec((s, LANE), lambda b, h, qi: (b, _K_COL + h)),
                  pl.BlockSpec((s, LANE), lambda b, h, qi: (b, _V_COL + h)),
                  pl.BlockSpec((None, 4, DA_DK), lambda b, h, qi: (layer, 0, 0)),
                  pl.BlockSpec((None, 1, DA_DV), lambda b, h, qi: (layer, 0, 0))],
        out_specs=pl.BlockSpec((s, LANE), lambda b, h, qi: (b, h)),
        out_shape=jax.ShapeDtypeStruct((geo.t_ctx, DA_W), BF16),
        scratch_shapes=_attn_scratch(s),
        compiler_params=_cparams(("arbitrary",) * 3),
        name="attention_context",
    )(z, z, z, da_lambda, da_norm_g)


def attention_latent(z, cache_k, cache_v, da_lambda, da_norm_g, layer, lam_init, geo):
    past = cache_k.shape[2]
    s = geo.s_lat
    n_kv = past + s
    nq = s // TQ
    q0 = geo.t_ctx // TQ
    k0 = geo.t_ctx // s
    return pl.pallas_call(
        functools.partial(_attn_kernel, past=past, n_new=s, kv_chunk=_kv_chunk(n_kv), lam_init=lam_init),
        grid=(geo.n_lat, DA_H, nq),
        in_specs=[pl.BlockSpec((TQ, LANE), lambda b, h, qi: (q0 + b * nq + qi, _Q_COL + h)),
                  pl.BlockSpec((None, None, past, LANE), lambda b, h, qi: (b, layer, 0, h)),
                  pl.BlockSpec((None, None, past, LANE), lambda b, h, qi: (b, layer, 0, h)),
                  pl.BlockSpec((s, LANE), lambda b, h, qi: (k0 + b, _K_COL + h)),
                  pl.BlockSpec((s, LANE), lambda b, h, qi: (k0 + b, _V_COL + h)),
                  pl.BlockSpec((None, 4, DA_DK), lambda b, h, qi: (layer, 0, 0)),
                  pl.BlockSpec((None, 1, DA_DV), lambda b, h, qi: (layer, 0, 0))],
        out_specs=pl.BlockSpec((TQ, LANE), lambda b, h, qi: (b * nq + qi, h)),
        out_shape=jax.ShapeDtypeStruct((geo.t_lat, DA_W), BF16),
        scratch_shapes=_attn_scratch(n_kv),
        compiler_params=_cparams(("arbitrary",) * 3),
        name="attention_latent",
    )(z, cache_k, cache_v, z, z, da_lambda, da_norm_g)


def _mlstm_kernel(*refs, geo, rev, n_chunk):
    if rev:
        (q_ref, k_ref, v_ref, gt_ref, gb_ref, c0_ref, n0_ref, m0_ref,
         h_ref, co_ref, no_ref, mo_ref, c_scr, n_scr, m_scr) = refs
    else:
        (q_ref, k_ref, v_ref, o_ref, hb_ref, gt_ref, gb_ref, ng_ref, c0_ref, n0_ref, m0_ref,
         h_ref, co_ref, no_ref, mo_ref, c_scr, n_scr, m_scr) = refs
    p = pl.program_id(0)
    g = n_chunk - 1 - p if rev else p
    _, t_first, t_last = _chunk_info(g, geo)
    starts = t_last if rev else t_first
    d = 1 if rev else 0

    @pl.when(starts)
    def _():
        c_scr[...] = c0_ref[...]
        n_scr[...] = n0_ref[...]
        m_scr[...] = m0_ref[...]

    gt = gt_ref[...] + gb_ref[...]
    eye = jnp.where(lax.broadcasted_iota(jnp.int32, (N_GATE, N_GATE), 0)
                    == lax.broadcasted_iota(jnp.int32, (N_GATE, N_GATE), 1), 1.0, 0.0)
    gtt = lax.dot_general(eye, gt, (((1,), (1,)), ((), ())), precision=HIGHEST, preferred_element_type=F32)
    ti = lax.broadcasted_iota(jnp.int32, (CHUNK, CHUNK), 0)
    si = lax.broadcasted_iota(jnp.int32, (CHUNK, CHUNK), 1)
    keep = (si >= ti) if rev else (si <= ti)
    tri = jnp.where(keep, 1.0, 0.0)
    b_col_all = jnp.dot(tri, _log_sigmoid(gt), precision=HIGHEST, preferred_element_type=F32)
    tri_t = jnp.where((ti >= si) if rev else (ti <= si), 1.0, 0.0)
    b_row_all = jnp.dot(_log_sigmoid(gtt), tri_t, precision=HIGHEST, preferred_element_type=F32)
    edge = 0 if rev else CHUNK - 1

    for h in range(ML_H):
        ci = d * 2 * ML_H + h
        cf = ci + ML_H
        hs = slice(h * ML_DH, (h + 1) * ML_DH)
        qb = q_ref[:, hs]
        vb = v_ref[:, hs]
        qh = qb.astype(F32)
        kh = k_ref[:, hs].astype(F32) * (ML_DH ** -0.5)
        vh = vb.astype(F32)
        kb = kh.astype(BF16)
        li_col = gt[:, ci:ci + 1]
        li_row = gtt[ci:ci + 1, :]
        b_col = b_col_all[:, cf:cf + 1]
        b_row = b_row_all[cf:cf + 1, :]
        m_prev = m_scr[h][:, 0:1]
        c_prev = c_scr[h]
        n_prev = n_scr[h]

        dm = jnp.where(keep, b_col - b_row + li_row, -jnp.inf)
        inter = b_col + m_prev
        m_t = jnp.maximum(inter, jnp.max(dm, axis=-1, keepdims=True))
        s = lax.dot_general(qb, kb, (((1,), (1,)), ((), ())), preferred_element_type=F32) * jnp.exp(dm - m_t)
        w_inter = jnp.exp(inter - m_t)
        cq = lax.dot_general(qb, c_prev.astype(BF16), (((1,), (1,)), ((), ())), preferred_element_type=F32)
        num = jnp.dot(s.astype(BF16), vb, preferred_element_type=F32) + w_inter * cq
        den = jnp.sum(s, axis=-1, keepdims=True) + w_inter * jnp.sum(qh * n_prev, axis=-1, keepdims=True)
        hh = num / jnp.maximum(jnp.abs(den), jnp.exp(-m_t))

        b_tot = b_col[edge:edge + 1, :]
        gg = b_tot - b_col + li_col
        m_new = jnp.maximum(b_tot + m_prev, jnp.max(gg, axis=0, keepdims=True))
        wgt = jnp.exp(gg - m_new)
        decay = jnp.exp(b_tot + m_prev - m_new)
        wv_t = (wgt * vh).T.astype(BF16)
        c_new = decay * c_prev + jnp.dot(wv_t, kb, preferred_element_type=F32)
        n_new = decay * n_prev + jnp.sum(wgt * kh, axis=0, keepdims=True)
        c_scr[h] = c_new
        n_scr[h] = n_new
        m_scr[h] = jnp.broadcast_to(m_new, (1, ML_DH))
        co_ref[h] = c_new
        no_ref[h] = n_new
        mo_ref[h] = jnp.broadcast_to(m_new, (1, ML_DH))

        if rev:
            h_ref[:, hs] = hh
        else:
            gated = jax.nn.sigmoid(o_ref[:, hs].astype(F32)) * (hh + hb_ref[:, hs])
            h_ref[:, hs] = (_rms(gated) * ng_ref[:, hs]).astype(BF16)


_MLQ_COL = (3 * DA_W) // ML_W


def mlstm_direction(z, hb, gates, gate_b, norm_g, c0, n0, m0, layer, geo, rev):
    n_chunk = geo.t // CHUNK
    d = 1 if rev else 0

    def gi(p):
        return n_chunk - 1 - p if rev else p

    def sid(p):
        return _chunk_info(gi(p), geo)[0]

    def zspec(col):
        return pl.BlockSpec((CHUNK, ML_W), lambda p: (gi(p), col))

    qkv = [zspec(_MLQ_COL), zspec(_MLQ_COL + 1), zspec(_MLQ_COL + 2)]
    gate_specs = [pl.BlockSpec((CHUNK, N_GATE), lambda p: (gi(p), 0)),
                  pl.BlockSpec((None, 1, N_GATE), lambda p: (layer, 0, 0))]
    st_specs = [pl.BlockSpec((None, None, ML_H, ML_DH, ML_DH), lambda p: (sid(p), d, 0, 0, 0)),
                pl.BlockSpec((None, None, ML_H, 1, ML_DH), lambda p: (sid(p), d, 0, 0, 0)),
                pl.BlockSpec((None, None, ML_H, 1, ML_DH), lambda p: (sid(p), d, 0, 0, 0))]
    if rev:
        in_specs = qkv + gate_specs + st_specs
        args = (z, z, z, gates, gate_b, c0, n0, m0)
        h_dtype = F32
    else:
        in_specs = (qkv + [zspec(_MLQ_COL + 3), pl.BlockSpec((CHUNK, ML_W), lambda p: (gi(p), 0))] + gate_specs
                    + [pl.BlockSpec((None, 1, ML_W), lambda p: (layer, 0, 0))] + st_specs)
        args = (z, z, z, z, hb, gates, gate_b, norm_g, c0, n0, m0)
        h_dtype = BF16
    return pl.pallas_call(
        functools.partial(_mlstm_kernel, geo=geo, rev=rev, n_chunk=n_chunk),
        grid=(n_chunk,),
        in_specs=in_specs,
        out_specs=[pl.BlockSpec((CHUNK, ML_W), lambda p: (gi(p), 0)),
                   pl.BlockSpec((None, ML_H, ML_DH, ML_DH), lambda p: (sid(p), 0, 0, 0)),
                   pl.BlockSpec((None, ML_H, 1, ML_DH), lambda p: (sid(p), 0, 0, 0)),
                   pl.BlockSpec((None, ML_H, 1, ML_DH), lambda p: (sid(p), 0, 0, 0))],
        out_shape=[jax.ShapeDtypeStruct((geo.t, ML_W), h_dtype),
                   jax.ShapeDtypeStruct((geo.n_seq, ML_H, ML_DH, ML_DH), F32),
                   jax.ShapeDtypeStruct((geo.n_seq, ML_H, 1, ML_DH), F32),
                   jax.ShapeDtypeStruct((geo.n_seq, ML_H, 1, ML_DH), F32)],
        scratch_shapes=[pltpu.VMEM((ML_H, ML_DH, ML_DH), F32),
                        pltpu.VMEM((ML_H, 1, ML_DH), F32),
                        pltpu.VMEM((ML_H, 1, ML_DH), F32)],
        compiler_params=_cparams(("arbitrary",)),
        name="mlstm_bwd" if rev else "mlstm_fwd",
    )(*args)


def _outproj_kernel(x_ref, rg_ref, dac_ref, dal_ref, ml_ref, w_ref, mod_ref, g_ref, wr_ref, br_ref,
                    x1_ref, h2_ref, te_ref, tw_ref, *, geo):
    i = pl.program_id(0)
    r = _cond_row(i, geo, TM_OUT)
    gate1 = mod_ref[pl.ds(r, 1), 2 * D:3 * D]
    shift2 = mod_ref[pl.ds(r, 1), 3 * D:4 * D]
    scale2 = mod_ref[pl.ds(r, 1), 4 * D:5 * D]
    da = jnp.where(i < geo.t_ctx // TM_OUT, dac_ref[...], dal_ref[...])
    y = jnp.dot(rg_ref[...], w_ref[0:RG_W, :], preferred_element_type=F32)
    y = y + jnp.dot(da, w_ref[RG_W:RG_W + DA_W, :], preferred_element_type=F32)
    y = y + jnp.dot(ml_ref[...], w_ref[RG_W + DA_W:D, :], preferred_element_type=F32)
    x1 = x_ref[...] + gate1 * y
    x1_ref[...] = x1
    h2 = _rms(x1) * g_ref[...] * (1.0 + scale2) + shift2
    lo = lax.bitcast_convert_type(h2[:, 0:D // 2].astype(BF16).astype(F32), jnp.uint32) >> 16
    hi = lax.bitcast_convert_type(h2[:, D // 2:D].astype(BF16).astype(F32), jnp.uint32)
    h2_ref[...] = hi | lo

    logits = jnp.dot(h2.astype(BF16), wr_ref[...], preferred_element_type=F32) + br_ref[...]
    lane = lax.broadcasted_iota(jnp.int32, logits.shape, 1)
    lane_f = lane.astype(F32)
    sel_e = jnp.zeros(logits.shape, jnp.int32)
    sel_v = jnp.zeros(logits.shape, F32)
    top0 = None
    total = None
    for k in range(TOP_K):
        mk = jnp.max(logits, axis=-1, keepdims=True)
        idx = jnp.min(jnp.where(logits == mk, lane_f, float(LANE)), axis=-1, keepdims=True)
        if k == 0:
            top0 = mk
        ek = jnp.exp(mk - top0)
        total = ek if total is None else total + ek
        sel_e = jnp.where(lane == k, idx.astype(jnp.int32), sel_e)
        sel_v = jnp.where(lane == k, ek, sel_v)
        logits = jnp.where(lane_f == idx, -jnp.inf, logits)
    te_ref[...] = sel_e
    tw_ref[...] = sel_v / total


def out_projection(x, out_rg, da_ctx, da_lat, out_ml, w_out_bf, mod, norm_g, router_w, router_b, layer, geo):
    n_i = geo.t // TM_OUT
    n_c = geo.t_ctx // TM_OUT
    return pl.pallas_call(
        functools.partial(_outproj_kernel, geo=geo),
        grid=(n_i,),
        in_specs=[pl.BlockSpec((TM_OUT, D), lambda i: (i, 0)),
                  pl.BlockSpec((TM_OUT, RG_W), lambda i: (i, 0)),
                  pl.BlockSpec((TM_OUT, DA_W), lambda i: (jnp.minimum(i, n_c - 1), 0)),
                  pl.BlockSpec((TM_OUT, DA_W), lambda i: (jnp.maximum(i - n_c, 0), 0)),
                  pl.BlockSpec((TM_OUT, ML_W), lambda i: (i, 0)),
                  pl.BlockSpec((None, D, D), lambda i: (layer, 0, 0), pipeline_mode=pl.Buffered(1)),
                  pl.BlockSpec((None, geo.cond_rows, 6 * D), lambda i: (layer, 0, 0), pipeline_mode=pl.Buffered(1)),
                  pl.BlockSpec((None, 1, D), lambda i: (layer, 0, 0)),
                  pl.BlockSpec((None, D, LANE), lambda i: (layer, 0, 0), pipeline_mode=pl.Buffered(1)),
                  pl.BlockSpec((None, 1, LANE), lambda i: (layer, 0, 0))],
        out_specs=[pl.BlockSpec((TM_OUT, D), lambda i: (i, 0)),
                   pl.BlockSpec((TM_OUT, D // 2), lambda i: (i, 0)),
                   pl.BlockSpec((TM_OUT, LANE), lambda i: (i, 0)),
                   pl.BlockSpec((TM_OUT, LANE), lambda i: (i, 0))],
        out_shape=[jax.ShapeDtypeStruct((geo.t, D), F32),
                   jax.ShapeDtypeStruct((geo.t, D // 2), jnp.uint32),
                   jax.ShapeDtypeStruct((geo.t, LANE), jnp.int32),
                   jax.ShapeDtypeStruct((geo.t, LANE), F32)],
        compiler_params=_cparams(("arbitrary",)),
        name="out_projection",
    )(x, out_rg, da_ctx, da_lat, out_ml, w_out_bf, mod, norm_g, router_w, router_b)


def _dispatch_kernel(dest_ref, h_ref, xs_in, xs_out, sem):
    del xs_in

    def body(r, carry):
        for k in range(TOP_K):
            d = dest_ref[r * TOP_K + k]
            pltpu.make_async_copy(h_ref.at[pl.ds(r, 1)], xs_out.at[pl.ds(d, 1)], sem).start(priority=k % 2)
        return carry

    lax.fori_loop(0, TM_DSP, body, 0)
    for k in range(TOP_K):
        pltpu.make_async_copy(h_ref, xs_out.at[pl.ds(0, TM_DSP)], sem).wait()


def moe_dispatch(dest, h2_packed, n_rows):
    t = h2_packed.shape[0]
    return pl.pallas_call(
        _dispatch_kernel,
        grid=(t // TM_DSP,),
        in_specs=[pl.BlockSpec((TM_DSP * TOP_K,), lambda i: (i,), memory_space=pltpu.SMEM),
                  pl.BlockSpec((TM_DSP, D // 2), lambda i: (i, 0)),
                  pl.BlockSpec(memory_space=pl.ANY)],
        out_specs=pl.BlockSpec(memory_space=pl.ANY),
        out_shape=jax.ShapeDtypeStruct((n_rows, D // 2), jnp.uint32),
        scratch_shapes=[pltpu.SemaphoreType.DMA(())],
        input_output_aliases={2: 0},
        compiler_params=_cparams(("arbitrary",), disable_bounds_checks=True),
        name="moe_dispatch",
    )(dest, h2_packed, jnp.zeros((n_rows, D // 2), jnp.uint32))


def _pack_bf16_pairs(lo_f32, hi_f32):
    lo = lax.bitcast_convert_type(lo_f32.astype(BF16).astype(F32), jnp.uint32) >> 16
    hi = lax.bitcast_convert_type(hi_f32.astype(BF16).astype(F32), jnp.uint32)
    return hi | lo


def _unpack_bf16_pairs(u):
    lo = lax.bitcast_convert_type(u << 16, F32)
    hi = lax.bitcast_convert_type((u >> 16) << 16, F32)
    return lo, hi


def _moe_kernel(be_ref, nu_ref, x_ref, bgu_ref, bdn_ref, wgu_hbm, wdn_hbm, o_ref, x_scr, act_scr, wbuf, sem, *, layer):
    m = pl.program_id(0)
    n_used = nu_ref[0]
    used = m < n_used
    half = D // 2
    n_a = FF // TF_MOE
    n_b = D // TN_MOE
    n_chunk = n_a + n_b

    def chunk_copies(blk, c, slot):
        e = be_ref[blk]
        if c < n_a:
            cols = pl.ds(c * TF_MOE, TF_MOE)
            ucols = pl.ds(FF + c * TF_MOE, TF_MOE)
            return [pltpu.make_async_copy(wgu_hbm.at[layer, e, :, cols], wbuf.at[slot, 0], sem.at[slot]),
                    pltpu.make_async_copy(wgu_hbm.at[layer, e, :, ucols], wbuf.at[slot, 1], sem.at[slot])]
        cols = pl.ds((c - n_a) * TN_MOE, TN_MOE)
        return [pltpu.make_async_copy(wdn_hbm.at[layer, e, :, cols], wbuf.at[slot, 0], sem.at[slot])]

    @pl.when(jnp.logical_not(used))
    def _():
        o_ref[...] = jnp.zeros(o_ref.shape, jnp.uint32)

    @pl.when(used)
    def _():
        @pl.when(m == 0)
        def _():
            for cp in chunk_copies(m, 0, 0):
                cp.start()

        lo, hi = _unpack_bf16_pairs(x_ref[...])
        x_scr[:, 0:half] = lo.astype(BF16)
        x_scr[:, half:D] = hi.astype(BF16)

        for c in range(n_chunk):
            slot = c % 2
            if c + 1 < n_chunk:
                for cp in chunk_copies(m, c + 1, (c + 1) % 2):
                    cp.start()
            else:
                @pl.when(m + 1 < n_used)
                def _():
                    for cp in chunk_copies(m + 1, 0, 0):
                        cp.start()
            for cp in chunk_copies(m, c, slot):
                cp.wait()

            if c < n_a:
                cols = slice(c * TF_MOE, (c + 1) * TF_MOE)
                x = x_scr[...]
                gate = jnp.dot(x, wbuf[slot, 0].astype(BF16), preferred_element_type=F32) + bgu_ref[:, cols]
                up = (jnp.dot(x, wbuf[slot, 1].astype(BF16), preferred_element_type=F32)
                      + bgu_ref[:, FF + c * TF_MOE:FF + (c + 1) * TF_MOE])
                gate = jnp.minimum(gate, SWIGLU_LIMIT)
                up = jnp.clip(up, -SWIGLU_LIMIT, SWIGLU_LIMIT)
                act_scr[:, cols] = (gate * jax.nn.sigmoid(SWIGLU_ALPHA * gate) * (up + 1.0)).astype(BF16)
            else:
                n = c - n_a
                y = (jnp.dot(act_scr[...], wbuf[slot, 0].astype(BF16), preferred_element_type=F32)
                     + bdn_ref[:, n * TN_MOE:(n + 1) * TN_MOE])
                o_ref[:, n * (TN_MOE // 2):(n + 1) * (TN_MOE // 2)] = _pack_bf16_pairs(
                    y[:, 0:TN_MOE // 2], y[:, TN_MOE // 2:TN_MOE])


def moe_experts(x_sorted, blk_e, n_used, w_gu, b_gu, w_dn, b_dn, layer):
    assert TF_MOE == TN_MOE and FF == D and ((FF // TF_MOE) + (D // TN_MOE)) % 2 == 0
    n_rows = x_sorted.shape[0]
    n_blk = n_rows // TM_MOE
    grid_spec = pltpu.PrefetchScalarGridSpec(
        num_scalar_prefetch=2,
        grid=(n_blk,),
        in_specs=[pl.BlockSpec((TM_MOE, D // 2), lambda m, be, nu: (m, 0)),
                  pl.BlockSpec((None, None, 1, 2 * FF), lambda m, be, nu: (layer, be[m], 0, 0)),
                  pl.BlockSpec((None, None, 1, D), lambda m, be, nu: (layer, be[m], 0, 0)),
                  pl.BlockSpec(memory_space=pl.ANY),
                  pl.BlockSpec(memory_space=pl.ANY)],
        out_specs=pl.BlockSpec((TM_MOE, D // 2), lambda m, be, nu: (m, 0)),
        scratch_shapes=[pltpu.VMEM((TM_MOE, D), BF16), pltpu.VMEM((TM_MOE, FF), BF16),
                        pltpu.VMEM((2, 2, D, TF_MOE), F32), pltpu.SemaphoreType.DMA((2,))],
    )
    n_layer, n_exp = w_gu.shape[:2]
    return pl.pallas_call(
        functools.partial(_moe_kernel, layer=layer),
        grid_spec=grid_spec,
        out_shape=jax.ShapeDtypeStruct((n_rows, D // 2), jnp.uint32),
        compiler_params=_cparams(("arbitrary",)),
        name="moe_experts",
    )(blk_e, n_used, x_sorted, b_gu.reshape(n_layer, n_exp, 1, 2 * FF), b_dn.reshape(n_layer, n_exp, 1, D),
      w_gu, w_dn)


def moe_routing(top_e):
    flat_e = top_e.reshape(-1)
    n_assign = flat_e.shape[0]
    onehot = (flat_e[:, None] == jnp.arange(N_EXP, dtype=jnp.int32)[None, :]).astype(jnp.int32)
    csum = jnp.cumsum(onehot, axis=0)
    counts = csum[-1]
    pos = jnp.take_along_axis(csum, flat_e[:, None], axis=1)[:, 0] - 1
    padded = (counts + TM_MOE - 1) // TM_MOE * TM_MOE
    pad_end = jnp.cumsum(padded)
    pad_start = pad_end - padded
    dest = pad_start[flat_e] + pos
    n_blk = moe_num_blocks(n_assign)
    blk_e = jnp.minimum(jnp.searchsorted(pad_end, jnp.arange(n_blk, dtype=jnp.int32) * TM_MOE, side='right'),
                        N_EXP - 1).astype(jnp.int32)
    n_used = (pad_end[-1] // TM_MOE).astype(jnp.int32).reshape(1)
    return dest.astype(jnp.int32), blk_e, n_used


def moe_num_blocks(n_assign):
    return -(-n_assign // TM_MOE) + N_EXP


def _combine_kernel(dfirst_ref, dnext_ref, x_ref, tw_ref, mod_ref, yb_ref, o_ref, buf, sem, *, geo, n_blk):
    i = pl.program_id(0)

    def gather(d_ref, slot):
        def body(r, carry):
            for k in range(TOP_K):
                d = d_ref[r * TOP_K + k]
                pltpu.make_async_copy(yb_ref.at[pl.ds(d, 1)], buf.at[slot, k, pl.ds(r, 1)],
                                      sem.at[slot]).start(priority=k % 2)
            return carry

        lax.fori_loop(0, TM_CMB, body, 0)

    @pl.when(i == 0)
    def _():
        gather(dfirst_ref, 0)

    @pl.when(i + 1 < n_blk)
    def _():
        gather(dnext_ref, (i + 1) % 2)

    slot = i % 2
    for k in range(TOP_K):
        pltpu.make_async_copy(yb_ref.at[pl.ds(0, TM_CMB)], buf.at[slot, k], sem.at[slot]).wait()

    r = _cond_row(i, geo, TM_CMB)
    tw = tw_ref[...]
    hw = TN_MOE // 2
    for n in range(D // TN_MOE):
        y_lo = None
        y_hi = None
        for k in range(TOP_K):
            lo, hi = _unpack_bf16_pairs(buf[slot, k, :, n * hw:(n + 1) * hw])
            w = tw[:, k:k + 1]
            y_lo = w * lo if y_lo is None else y_lo + w * lo
            y_hi = w * hi if y_hi is None else y_hi + w * hi
        c0 = n * TN_MOE
        o_ref[:, c0:c0 + hw] = x_ref[:, c0:c0 + hw] + mod_ref[pl.ds(r, 1), 5 * D + c0:5 * D + c0 + hw] * y_lo
        o_ref[:, c0 + hw:c0 + 2 * hw] = (x_ref[:, c0 + hw:c0 + 2 * hw]
                                         + mod_ref[pl.ds(r, 1), 5 * D + c0 + hw:5 * D + c0 + 2 * hw] * y_hi)


def moe_combine(dest, x1, top_w, yb, mod, layer, geo):
    n_blk = geo.t // TM_CMB
    dblk = TM_CMB * TOP_K
    return pl.pallas_call(
        functools.partial(_combine_kernel, geo=geo, n_blk=n_blk),
        grid=(n_blk,),
        in_specs=[pl.BlockSpec((dblk,), lambda i: (0,), memory_space=pltpu.SMEM),
                  pl.BlockSpec((dblk,), lambda i: (jnp.minimum(i + 1, n_blk - 1),), memory_space=pltpu.SMEM),
                  pl.BlockSpec((TM_CMB, D), lambda i: (i, 0)),
                  pl.BlockSpec((TM_CMB, LANE), lambda i: (i, 0)),
                  pl.BlockSpec((None, geo.cond_rows, 6 * D), lambda i: (layer, 0, 0)),
                  pl.BlockSpec(memory_space=pl.ANY)],
        out_specs=pl.BlockSpec((TM_CMB, D), lambda i: (i, 0)),
        out_shape=jax.ShapeDtypeStruct((geo.t, D), F32),
        scratch_shapes=[pltpu.VMEM((2, TOP_K, TM_CMB, D // 2), jnp.uint32), pltpu.SemaphoreType.DMA((2,))],
        compiler_params=_cparams(("arbitrary",), disable_bounds_checks=True),
        name="moe_combine",
    )(dest, dest, x1, top_w, mod, yb)


def _final_norm_kernel(x_ref, g_ref, o_ref):
    o_ref[...] = _rms(x_ref[...]) * g_ref[...]


def final_norm(x, g):
    t = x.shape[0]
    return pl.pallas_call(
        _final_norm_kernel,
        grid=(t // TM_EW,),
        in_specs=[pl.BlockSpec((TM_EW, D), lambda i: (i, 0)), pl.BlockSpec((1, D), lambda i: (0, 0))],
        out_specs=pl.BlockSpec((TM_EW, D), lambda i: (i, 0)),
        out_shape=jax.ShapeDtypeStruct((t, D), F32),
        compiler_params=_cparams(("arbitrary",)),
        name="final_norm",
    )(x, g.reshape(1, D))


def _block_diag(w):
    eye = jnp.eye(RG_BLOCKS, dtype=w.dtype)
    dense = w[..., :, :, None, :] * eye[:, None, :, None]
    return dense.reshape(*w.shape[:-3], RG_W, RG_W)


def kernel(x_prompt, x_sample, c, cache_k, cache_v, state_rglru, state_mlstm_C, state_mlstm_n, state_mlstm_m, c_ctx, ada_w, ada_b, norm1_g, norm2_g, w_in, rg_conv_w, rg_conv_b, rg_gate_w, rg_gate_b, rg_lambda, da_lambda, da_norm_g, ml_gate_b, ml_norm_g, w_out, router_w, router_b, moe_w_gu, moe_b_gu, moe_w_down, moe_b_down, final_g):
    n_ctx, s_ctx, _ = x_prompt.shape
    n_lat, s_lat, _ = x_sample.shape
    n_layer = ada_w.shape[0]
    past = cache_k.shape[2]
    geo = Geo(n_ctx, s_ctx, n_lat, s_lat)
    assert geo.t_ctx % TM_IN == 0 and s_lat % TM_IN == 0 and s_ctx % CHUNK == 0 and s_lat % CHUNK == 0
    assert s_lat % TQ == 0 and geo.t_ctx % s_lat == 0 and past % KV_STEP == 0 and s_ctx % SUBLANE == 0

    x = jnp.concatenate([x_prompt.reshape(geo.t_ctx, D), x_sample.reshape(geo.t_lat, D)], axis=0)
    cond = jnp.zeros((geo.cond_rows, D), F32).at[:n_lat].set(c).at[n_lat].set(c_ctx)
    mod = ada_modulation(cond, ada_w, ada_b)

    w_in_bf = w_in.astype(BF16)
    w_gate_bf = w_in_bf[:, :, N_MAIN:]
    w_out_bf = w_out.astype(BF16)
    rg_gate_dense = jnp.concatenate([_block_diag(rg_gate_w[:, :, 0]), _block_diag(rg_gate_w[:, :, 1])],
                                    axis=-1).astype(BF16)
    rg_gate_bias = rg_gate_b.reshape(n_layer, 2, 1, 2 * RG_W)
    rg_lam = rg_lambda.reshape(n_layer, 2, 1, RG_W)
    rg_cb = rg_conv_b.reshape(n_layer, 1, RG_W)
    ml_gb = ml_gate_b.reshape(n_layer, 1, N_GATE)
    ml_ng = ml_norm_g.reshape(n_layer, 1, ML_W)
    da_ng = da_norm_g.reshape(n_layer, 1, DA_DV)
    n1g = norm1_g.reshape(n_layer, 1, D)
    n2g = norm2_g.reshape(n_layer, 1, D)
    router_w_pad = jnp.pad(router_w, ((0, 0), (0, 0), (0, LANE - N_EXP))).astype(BF16)
    router_b_pad = jnp.pad(router_b, ((0, 0), (0, LANE - N_EXP)), constant_values=-1e30).reshape(n_layer, 1, LANE)
    rope_cos, rope_sin = rope_tables(s_lat)
    cache_k4 = cache_k.reshape(n_lat, n_layer, past, DA_W)
    cache_v4 = cache_v.reshape(n_lat, n_layer, past, DA_W)

    ks, vs, rgs, cs, ns, ms = [], [], [], [], [], []
    for l in range(n_layer):
        lam_init = 0.8 - 0.6 * math.exp(-0.3 * l)
        z_rg, z, kv_ctx, gates = in_projection(x, mod, n1g, w_in_bf, w_gate_bf, rope_cos, rope_sin, l, geo)

        rg_h0 = jnp.concatenate([jnp.zeros((n_ctx, 2, RG_W), F32), state_rglru[:, l]], axis=0).reshape(geo.n_seq, 2, 1, RG_W)
        hb, rg_sb = rglru_direction(z_rg, None, rg_conv_w, rg_cb, rg_gate_dense, rg_gate_bias, rg_lam, rg_h0, l, geo, True)
        out_rg, rg_sf = rglru_direction(z_rg, hb, rg_conv_w, rg_cb, rg_gate_dense, rg_gate_bias, rg_lam, rg_h0, l, geo, False)

        da_ctx = attention_context(z, da_lambda, da_ng, l, lam_init, geo)
        da_lat = attention_latent(z, cache_k4, cache_v4, da_lambda, da_ng, l, lam_init, geo)

        c0 = jnp.concatenate([jnp.zeros((n_ctx, 2, ML_H, ML_DH, ML_DH), F32), state_mlstm_C[:, l]], axis=0)
        n0 = jnp.concatenate([jnp.zeros((n_ctx, 2, ML_H, ML_DH), F32), state_mlstm_n[:, l]], axis=0)
        n0 = n0.reshape(geo.n_seq, 2, ML_H, 1, ML_DH)
        m0 = jnp.concatenate([jnp.zeros((n_ctx, 2, ML_H), F32), state_mlstm_m[:, l]], axis=0)
        m0 = jnp.broadcast_to(m0[..., None, None], (geo.n_seq, 2, ML_H, 1, ML_DH))
        mhb, cb, nb, mb = mlstm_direction(z, None, gates, ml_gb, ml_ng, c0, n0, m0, l, geo, True)
        out_ml, cf, nf, mf = mlstm_direction(z, mhb, gates, ml_gb, ml_ng, c0, n0, m0, l, geo, False)

        x1, h2_packed, top_e, top_w = out_projection(x, out_rg, da_ctx, da_lat, out_ml, w_out_bf, mod, n2g,
                                                     router_w_pad, router_b_pad, l, geo)
        dest, blk_e, n_used = moe_routing(top_e[:, :TOP_K])
        x_sorted = moe_dispatch(dest, h2_packed, moe_num_blocks(geo.t * TOP_K) * TM_MOE)
        yb = moe_experts(x_sorted, blk_e, n_used, moe_w_gu, moe_b_gu, moe_w_down, moe_b_down, l)
        x = moe_combine(dest, x1, top_w, yb, mod, l, geo)

        ks.append(kv_ctx[:, 0:DA_W].reshape(n_ctx, s_ctx, DA_H, DA_DV))
        vs.append(kv_ctx[:, DA_W:2 * DA_W].reshape(n_ctx, s_ctx, DA_H, DA_DV))
        rgs.append(jnp.stack([rg_sf[:n_ctx, 0], rg_sb[:n_ctx, 0]], axis=1))
        cs.append(jnp.stack([cf[:n_ctx], cb[:n_ctx]], axis=1))
        ns.append(jnp.stack([nf[:n_ctx, :, 0], nb[:n_ctx, :, 0]], axis=1))
        ms.append(jnp.stack([mf[:n_ctx, :, 0, 0], mb[:n_ctx, :, 0, 0]], axis=1))

    y = final_norm(x, final_g)
    y_prompt = y[:geo.t_ctx].reshape(n_ctx, s_ctx, D)
    y_sample = y[geo.t_ctx:].reshape(n_lat, s_lat, D)
    return (y_prompt, y_sample, jnp.stack(ks, axis=1), jnp.stack(vs, axis=1), jnp.stack(rgs, axis=1),
            jnp.stack(cs, axis=1), jnp.stack(ns, axis=1), jnp.stack(ms, axis=1))
```
